```python
import math
import jax, jax.numpy as jnp
from jax import lax
import numpy as np

D_MODEL = 1024
BATCH = 8
SEQ = 2048
DEPTH = 1
DEC_BATCH = 8
DEC_SEQ = 4096
PAST_LEN = 128

POOL_WINDOWS = (2, 4, 8, 16)
N_POOL_GROUPS = 4
POOL_GROUP_DIM = D_MODEL // 8
POOL_DIM = N_POOL_GROUPS * POOL_GROUP_DIM
N_ATT_HEADS = 8
ATT_HEAD_DIM = D_MODEL // (2 * N_ATT_HEADS)
ATT_V_DIM = 2 * ATT_HEAD_DIM
ATT_DIM = N_ATT_HEADS * ATT_V_DIM
QKV_DIM = 3 * ATT_DIM
IN_DIM = POOL_DIM + QKV_DIM
ROPE_THETA = 10000.0
Q_BLOCK = 128
N_BRANCHES = 2
PEER_HEADS = 8
PEER_KEYS = 128
PEER_EXPERTS = PEER_KEYS * PEER_KEYS
PEER_TOPK = 16
PEER_KEY_DIM = 128
PEER_HALF = PEER_KEY_DIM // 2
TOKEN_CHUNK = 128
NORM_EPS = 1e-6

kernel_name = "hybrid_pool_diffattn_peer_encoder"


def _rmsnorm(x, g):
    xf = x.astype(jnp.float32)
    y = xf * lax.rsqrt(jnp.mean(xf * xf, axis=-1, keepdims=True) + NORM_EPS)
    return (y * g.astype(jnp.float32)).astype(x.dtype)


def _rope_tables(S, dtype):
    inv_freq = 1.0 / (ROPE_THETA ** (jnp.arange(0, ATT_HEAD_DIM, 2, dtype=jnp.float32) / ATT_HEAD_DIM))
    ang = jnp.arange(S, dtype=jnp.float32)[:, None] * inv_freq[None, :]
    ang = jnp.concatenate([ang, ang], axis=-1)[:, None, :]
    return jnp.cos(ang).astype(dtype), jnp.sin(ang).astype(dtype)


def _apply_rope(t, cos, sin):
    t1, t2 = jnp.split(t, 2, axis=-1)
    return t * cos + jnp.concatenate([-t2, t1], axis=-1) * sin


def _multiscale_pool(p, pool_w, pool_scale):
    B, S, _ = p.shape
    pg = p.reshape(B, S, N_POOL_GROUPS, POOL_GROUP_DIM)
    csum = jnp.cumsum(pg.astype(jnp.float32), axis=1)
    csum = jnp.concatenate([jnp.zeros_like(csum[:, :1]), csum], axis=1)
    pos = jnp.arange(S, dtype=jnp.int32)[None, :]
    win = jnp.asarray(POOL_WINDOWS, dtype=jnp.int32)[:, None]
    lo = jnp.clip(pos - win // 2, 0, S)
    hi = jnp.clip(pos + win - win // 2, 0, S)
    gidx = jnp.arange(N_POOL_GROUPS, dtype=jnp.int32)[:, None]
    wsum = csum[:, hi, gidx] - csum[:, lo, gidx]
    cnt = (hi - lo).astype(jnp.float32)[None, :, :, None]
    mean = jnp.transpose(wsum / cnt, (0, 2, 1, 3)).astype(p.dtype)
    mixed = jnp.einsum('bsgc,gce->bsge', mean - pg, pool_w)
    return mixed.reshape(B, S, POOL_DIM) * pool_scale


def _diff_attention(qkv, cos, sin, lam, g_sub, lam_init):
    B, S, _ = qkv.shape
    q, k, v = jnp.split(qkv, 3, axis=-1)
    q = _apply_rope(q.reshape(B, S, 2 * N_ATT_HEADS, ATT_HEAD_DIM), cos, sin)
    k = _apply_rope(k.reshape(B, S, 2 * N_ATT_HEADS, ATT_HEAD_DIM), cos, sin)
    v = v.reshape(B, S, N_ATT_HEADS, ATT_V_DIM)
    nb = S // Q_BLOCK
    qb = jnp.transpose(q.reshape(B, nb, Q_BLOCK, 2 * N_ATT_HEADS, ATT_HEAD_DIM), (1, 0, 2, 3, 4))
    scale = ATT_HEAD_DIM ** -0.5

    def block(q_blk):
        s = jnp.einsum('bqhd,bkhd->bhqk', q_blk, k).astype(jnp.float32) * scale
        pr = jax.nn.softmax(s, axis=-1).reshape(B, N_ATT_HEADS, 2, Q_BLOCK, S)
        a = pr[:, :, 0] - lam * pr[:, :, 1]
        return jnp.einsum('bhqk,bkhe->bqhe', a.astype(v.dtype), v)

    o = lax.map(block, qb)
    o = jnp.transpose(o, (1, 0, 2, 3, 4)).reshape(B, S, N_ATT_HEADS, ATT_V_DIM)
    o = _rmsnorm(o, g_sub) * (1.0 - lam_init)
    return o.reshape(B, S, ATT_DIM)


def _peer(h, w_q, keys, u_tab, v_tab):
    B, S, D = h.shape
    hc = h.reshape(-1, TOKEN_CHUNK, D)

    def chunk(xc):
        q = (xc @ w_q).reshape(TOKEN_CHUNK, PEER_HEADS, 2, PEER_HALF)
        s = jnp.einsum('thpd,hpnd->thpn', q, keys).astype(jnp.float32)
        top_s, top_i = lax.top_k(s, PEER_TOPK)
        cand = top_s[:, :, 0, :, None] + top_s[:, :, 1, None, :]
        best_s, best_j = lax.top_k(cand.reshape(TOKEN_CHUNK, PEER_HEADS, PEER_TOPK * PEER_TOPK), PEER_TOPK)
        i1 = jnp.take_along_axis(top_i[:, :, 0], best_j // PEER_TOPK, axis=-1)
        i2 = jnp.take_along_axis(top_i[:, :, 1], best_j % PEER_TOPK, axis=-1)
        expert = i1 * PEER_KEYS + i2
        g = jax.nn.softmax(best_s, axis=-1)
        u = jnp.take(u_tab, expert, axis=0)
        act = jax.nn.gelu(jnp.einsum('thkd,td->thk', u, xc).astype(jnp.float32), approximate=False) * g
        vv = jnp.take(v_tab, expert, axis=0)
        return jnp.einsum('thk,thkd->td', act.astype(vv.dtype), vv)

    return lax.map(chunk, hc).reshape(B, S, D)


def _trunk(x, c, w_ada, b_ada, g_norm1, w_in, pool_w, pool_scale, w_pool_out,
           lam_q1, lam_k1, lam_q2, lam_k2, g_sub, w_att_out, w_gate, b_gate, w_out,
           g_norm2, w_peer_q, peer_keys, peer_u, peer_v, g_final):
    B, S, D = x.shape
    cos, sin = _rope_tables(S, x.dtype)
    for l in range(DEPTH):
        lam_init = 0.8 - 0.6 * math.exp(-0.3 * l)
        mod = jax.nn.silu(c) @ w_ada[l] + b_ada[l]
        sh1, sc1, gt1, sh2, sc2, gt2 = [m[:, None, :] for m in jnp.split(mod, 6, axis=-1)]
        h = _rmsnorm(x, g_norm1[l]) * (1 + sc1) + sh1
        proj = h @ w_in[l]
        y_pool = _multiscale_pool(proj[..., :POOL_DIM], pool_w[l], pool_scale[l]) @ w_pool_out[l]
        lam = (jnp.exp(jnp.sum(lam_q1[l].astype(jnp.float32) * lam_k1[l].astype(jnp.float32)))
               - jnp.exp(jnp.sum(lam_q2[l].astype(jnp.float32) * lam_k2[l].astype(jnp.float32)))
               + lam_init)
        y_att = _diff_attention(proj[..., POOL_DIM:], cos, sin, lam, g_sub[l], lam_init) @ w_att_out[l]
        gates = jax.nn.sigmoid(h @ w_gate[l] + b_gate[l])
        g_pool, g_att = jnp.split(gates, N_BRANCHES, axis=-1)
        x = x + gt1 * ((g_pool * y_pool + g_att * y_att) @ w_out[l])
        h2 = _rmsnorm(x, g_norm2[l]) * (1 + sc2) + sh2
        x = x + gt2 * _peer(h2, w_peer_q[l], peer_keys[l], peer_u[l], peer_v[l])
    return _rmsnorm(x, g_final)


def setup_inputs(seed: int = 0) -> dict:
    key = jax.random.key(seed)
    ks = jax.random.split(key, 32)
    f32 = jnp.float32
    nrm = lambda k, shape, s: jax.random.normal(k, shape, dtype=f32) * s
    L, D = DEPTH, D_MODEL
    return {
        "x_prompt": nrm(ks[0], (BATCH, SEQ, D), 1.0),
        "x_sample": nrm(ks[1], (DEC_BATCH, DEC_SEQ, D), 1.0),
        "c_prompt": nrm(ks[2], (BATCH, D), 1.0),
        "c_sample": nrm(ks[3], (DEC_BATCH, D), 1.0),
        "w_ada": nrm(ks[4], (L, D, 6 * D), D ** -0.5),
        "b_ada": nrm(ks[5], (L, 6 * D), 0.02),
        "g_norm1": 1.0 + nrm(ks[6], (L, D), 0.02),
        "w_in": nrm(ks[7], (L, D, IN_DIM), D ** -0.5),
        "pool_w": nrm(ks[8], (L, N_POOL_GROUPS, POOL_GROUP_DIM, POOL_GROUP_DIM), POOL_GROUP_DIM ** -0.5),
        "pool_scale": 1.0 + nrm(ks[9], (L, POOL_DIM), 0.1),
        "w_pool_out": nrm(ks[10], (L, POOL_DIM, D), POOL_DIM ** -0.5),
        "lam_q1": nrm(ks[11], (L, ATT_HEAD_DIM), 0.1),
        "lam_k1": nrm(ks[12], (L, ATT_HEAD_DIM), 0.1),
        "lam_q2": nrm(ks[13], (L, ATT_HEAD_DIM), 0.1),
        "lam_k2": nrm(ks[14], (L, ATT_HEAD_DIM), 0.1),
        "g_sub": 1.0 + nrm(ks[15], (L, ATT_V_DIM), 0.02),
        "w_att_out": nrm(ks[16], (L, ATT_DIM, D), ATT_DIM ** -0.5),
        "w_gate": nrm(ks[17], (L, D, N_BRANCHES * D), D ** -0.5),
        "b_gate": nrm(ks[18], (L, N_BRANCHES * D), 0.02),
        "w_out": nrm(ks[19], (L, D, D), D ** -0.5),
        "g_norm2": 1.0 + nrm(ks[20], (L, D), 0.02),
        "w_peer_q": nrm(ks[21], (L, D, PEER_HEADS * PEER_KEY_DIM), D ** -0.5),
        "peer_keys": nrm(ks[22], (L, PEER_HEADS, 2, PEER_KEYS, PEER_HALF), PEER_HALF ** -0.5),
        "peer_u": nrm(ks[23], (L, PEER_EXPERTS, D), D ** -0.5),
        "peer_v": nrm(ks[24], (L, PEER_EXPERTS, D), PEER_HEADS ** -0.5),
        "g_final": 1.0 + nrm(ks[25], (D,), 0.02),
    }


def reference(x_prompt, x_sample, c_prompt, c_sample, w_ada, b_ada, g_norm1, w_in, pool_w,
              pool_scale, w_pool_out, lam_q1, lam_k1, lam_q2, lam_k2, g_sub, w_att_out,
              w_gate, b_gate, w_out, g_norm2, w_peer_q, peer_keys, peer_u, peer_v, g_final):
    y_prompt = _trunk(x_prompt, c_prompt, w_ada, b_ada, g_norm1, w_in, pool_w, pool_scale,
                      w_pool_out, lam_q1, lam_k1, lam_q2, lam_k2, g_sub, w_att_out, w_gate,
                      b_gate, w_out, g_norm2, w_peer_q, peer_keys, peer_u, peer_v, g_final)
    y_sample = _trunk(x_sample, c_sample, w_ada, b_ada, g_norm1, w_in, pool_w, pool_scale,
                      w_pool_out, lam_q1, lam_k1, lam_q2, lam_k2, g_sub, w_att_out, w_gate,
                      b_gate, w_out, g_norm2, w_peer_q, peer_keys, peer_u, peer_v, g_final)
    return (y_prompt, y_sample)
```

```python
import functools
import math

import jax
import jax.numpy as jnp
from jax import lax
from jax.experimental import pallas as pl
from jax.experimental.pallas import tpu as pltpu

F32 = jnp.float32
BF16 = jnp.bfloat16
I32 = jnp.int32

NORM_EPS = 1e-6
ROPE_THETA = 10000.0
POOL_WINDOWS = (2, 4, 8, 16)
POOL_PAD = 16
N_HEADS = 8
HEAD_DIM = 64
LANES = 128
TOPK = 16
CAND_ROWS = tuple((a, TOPK // (a + 1)) for a in range(TOPK))
N_CAND = sum(n for _, n in CAND_ROWS)
N_CAND_PAD = 56
VMEM_LIMIT = 56 * 1024 * 1024
INV_SQRT2 = 1.0 / math.sqrt(2.0)
NEG_INF = float("-inf")


def _cparams(sem):
    return pltpu.CompilerParams(dimension_semantics=sem, vmem_limit_bytes=VMEM_LIMIT)


def _rms(x):
    return x * lax.rsqrt(jnp.mean(x * x, axis=-1, keepdims=True) + NORM_EPS)


def _ada_kernel(c_ref, w_ref, b_ref, o_ref):
    c = c_ref[...]
    a = (c * jax.nn.sigmoid(c)).astype(BF16)
    o_ref[...] = jnp.dot(a, w_ref[...], preferred_element_type=F32) + b_ref[...]


def _ada(c, w_bf, b):
    n, d = c.shape
    nout = w_bf.shape[1]
    return pl.pallas_call(
        _ada_kernel,
        grid=(nout // d,),
        in_specs=[pl.BlockSpec((n, d), lambda j: (0, 0)),
                  pl.BlockSpec((d, d), lambda j: (0, j)),
                  pl.BlockSpec((1, d), lambda j: (0, j))],
        out_specs=pl.BlockSpec((n, d), lambda j: (0, j)),
        out_shape=jax.ShapeDtypeStruct((n, nout), F32),
        compiler_params=_cparams(("parallel",)),
        name="ada",
    )(c, w_bf, b)


def _front_kernel(x_ref, sc_ref, sh_ref, g_ref, w_ref, cos_ref, sa_ref, sb_ref,
                  p_ref, q_ref, k_ref, v_ref, *, d, pool_dim):
    x = x_ref[0]
    h = (_rms(x) * g_ref[...]) * (1.0 + sc_ref[0]) + sh_ref[0]
    proj = jnp.dot(h.astype(BF16), w_ref[...], preferred_element_type=F32)
    p_ref[0] = proj[:, :pool_dim]
    cos, sa, sb = cos_ref[...], sa_ref[...], sb_ref[...]
    half = HEAD_DIM // 2

    def rope(t):
        return (t * cos + pltpu.roll(t, LANES - half, axis=1) * sa
                + pltpu.roll(t, half, axis=1) * sb)

    qscale = HEAD_DIM ** -0.5
    for j in range(d // LANES):
        o = pool_dim + j * LANES
        q_ref[0, :, j * LANES:(j + 1) * LANES] = (rope(proj[:, o:o + LANES]) * qscale).astype(BF16)
        o = pool_dim + d + j * LANES
        k_ref[0, :, j * LANES:(j + 1) * LANES] = rope(proj[:, o:o + LANES]).astype(BF16)
    v_ref[0] = proj[:, pool_dim + 2 * d:].astype(BF16)


def _front(x, sc, sh, g, w_in_bf, cos, sa, sb, tm):
    b, s, d = x.shape
    in_dim = w_in_bf.shape[1]
    pool_dim = in_dim - 3 * d
    row = lambda i, j: (i, j, 0)
    per_b = lambda i, j: (i, 0, 0)
    const = lambda i, j: (0, 0)
    return pl.pallas_call(
        functools.partial(_front_kernel, d=d, pool_dim=pool_dim),
        grid=(b, s // tm),
        in_specs=[pl.BlockSpec((1, tm, d), row),
                  pl.BlockSpec((1, 1, d), per_b),
                  pl.BlockSpec((1, 1, d), per_b),
                  pl.BlockSpec((1, d), const),
                  pl.BlockSpec((d, in_dim), const),
                  pl.BlockSpec((tm, LANES), lambda i, j: (j, 0)),
                  pl.BlockSpec((tm, LANES), lambda i, j: (j, 0)),
                  pl.BlockSpec((tm, LANES), lambda i, j: (j, 0))],
        out_specs=[pl.BlockSpec((1, tm, pool_dim), row),
                   pl.BlockSpec((1, tm, d), row),
                   pl.BlockSpec((1, tm, d), row),
                   pl.BlockSpec((1, tm, d), row)],
        out_shape=[jax.ShapeDtypeStruct((b, s, pool_dim), F32),
                   jax.ShapeDtypeStruct((b, s, d), BF16),
                   jax.ShapeDtypeStruct((b, s, d), BF16),
                   jax.ShapeDtypeStruct((b, s, d), BF16)],
        compiler_params=_cparams(("parallel", "parallel")),
        name="front",
    )(x, sc, sh, g, w_in_bf, cos, sa, sb)


def _pool_kernel(p_ref, w_ref, scale_ref, o_ref, pad_ref, *, s):
    pos = lax.broadcasted_iota(I32, (s, LANES), 0)
    zeros = jnp.zeros((POOL_PAD, LANES), F32)
    for g, win in enumerate(POOL_WINDOWS):
        half = win // 2
        xg = p_ref[0, :, g * LANES:(g + 1) * LANES]
        pad_ref[0:POOL_PAD, :] = zeros
        pad_ref[POOL_PAD:POOL_PAD + s, :] = xg
        pad_ref[POOL_PAD + s:2 * POOL_PAD + s, :] = zeros
        wsum = pad_ref[POOL_PAD - half:POOL_PAD - half + s, :]
        for dlt in range(-half + 1, half):
            wsum = wsum + pad_ref[POOL_PAD + dlt:POOL_PAD + dlt + s, :]
        cnt = (jnp.minimum(pos + half, s) - jnp.maximum(pos - half, 0)).astype(F32)
        diff = wsum / cnt - xg
        mixed = jnp.dot(diff.astype(BF16), w_ref[g], preferred_element_type=F32)
        o_ref[0, :, g * LANES:(g + 1) * LANES] = (
            mixed * scale_ref[:, g * LANES:(g + 1) * LANES]).astype(BF16)


def _pool(p, pool_w_bf, pool_scale):
    b, s, pool_dim = p.shape
    ng = pool_w_bf.shape[0]
    return pl.pallas_call(
        functools.partial(_pool_kernel, s=s),
        grid=(b,),
        in_specs=[pl.BlockSpec((1, s, pool_dim), lambda i: (i, 0, 0)),
                  pl.BlockSpec((ng, LANES, LANES), lambda i: (0, 0, 0)),
                  pl.BlockSpec((1, pool_dim), lambda i: (0, 0))],
        out_specs=pl.BlockSpec((1, s, pool_dim), lambda i: (i, 0, 0)),
        out_shape=jax.ShapeDtypeStruct((b, s, pool_dim), BF16),
        scratch_shapes=[pltpu.VMEM((s + 2 * POOL_PAD, LANES), F32)],
        compiler_params=_cparams(("parallel",)),
        name="pool",
    )(p, pool_w_bf, pool_scale)


def _attn_kernel(lamv_ref, q_ref, k_ref, v_ref, gsub_ref, o_ref, *, lam_init):
    lv = lamv_ref[...]
    lam = (jnp.exp(jnp.sum(lv[0:1] * lv[1:2], axis=-1, keepdims=True))
           - jnp.exp(jnp.sum(lv[2:3] * lv[3:4], axis=-1, keepdims=True)) + lam_init)
    q = q_ref[0]
    k = k_ref[0]
    first = lax.broadcasted_iota(I32, (1, LANES), 1) < HEAD_DIM
    zero = jnp.zeros_like(q)
    nt = (((1,), (1,)), ((), ()))
    s1 = lax.dot_general(jnp.where(first, q, zero), k, nt, preferred_element_type=F32)
    s2 = lax.dot_general(jnp.where(first, zero, q), k, nt, preferred_element_type=F32)
    p1 = jnp.exp(s1 - jnp.max(s1, axis=-1, keepdims=True))
    p2 = jnp.exp(s2 - jnp.max(s2, axis=-1, keepdims=True))
    c1 = 1.0 / jnp.sum(p1, axis=-1, keepdims=True)
    c2 = lam / jnp.sum(p2, axis=-1, keepdims=True)
    a = p1 * c1 - p2 * c2
    o = jnp.dot(a.astype(BF16), v_ref[0], preferred_element_type=F32)
    o_ref[0] = ((_rms(o) * gsub_ref[...]) * (1.0 - lam_init)).astype(BF16)


def _attention(q, k, v, lamv, g_sub, lam_init, tq):
    b, s, d = q.shape
    nh = d // LANES
    return pl.pallas_call(
        functools.partial(_attn_kernel, lam_init=lam_init),
        grid=(b, nh, s // tq),
        in_specs=[pl.BlockSpec((4, HEAD_DIM), lambda i, h, j: (0, 0)),
                  pl.BlockSpec((1, tq, LANES), lambda i, h, j: (i, j, h)),
                  pl.BlockSpec((1, s, LANES), lambda i, h, j: (i, 0, h)),
                  pl.BlockSpec((1, s, LANES), lambda i, h, j: (i, 0, h)),
                  pl.BlockSpec((1, LANES), lambda i, h, j: (0, 0))],
        out_specs=pl.BlockSpec((1, tq, LANES), lambda i, h, j: (i, j, h)),
        out_shape=jax.ShapeDtypeStruct((b, s, d), BF16),
        compiler_params=_cparams(("parallel", "parallel", "arbitrary")),
        name="attn",
    )(lamv, q, k, v, g_sub)


def _merge_kernel(x_ref, mixed_ref, att_ref, sc1_ref, sh1_ref, gt1_ref, sc2_ref, sh2_ref,
                  g1_ref, g2_ref, wpool_ref, watt_ref, wgate_ref, bgate_ref, wout_ref,
                  x1_ref, h2_ref, *, d):
    x = x_ref[0]
    h = ((_rms(x) * g1_ref[...]) * (1.0 + sc1_ref[0]) + sh1_ref[0]).astype(BF16)
    gates = jax.nn.sigmoid(jnp.dot(h, wgate_ref[...], preferred_element_type=F32) + bgate_ref[...])
    y_pool = jnp.dot(mixed_ref[0], wpool_ref[...], preferred_element_type=F32)
    y_att = jnp.dot(att_ref[0], watt_ref[...], preferred_element_type=F32)
    merged = gates[:, :d] * y_pool + gates[:, d:] * y_att
    x1 = x + gt1_ref[0] * jnp.dot(merged.astype(BF16), wout_ref[...], preferred_element_type=F32)
    x1_ref[0] = x1
    h2 = (_rms(x1) * g2_ref[...]) * (1.0 + sc2_ref[0]) + sh2_ref[0]
    h2_ref[0] = h2.astype(BF16)


def _merge(x, mixed, att, mods, g1, g2, wpool, watt, wgate, bgate, wout, tm):
    b, s, d = x.shape
    pool_dim = mixed.shape[-1]
    row = lambda i, j: (i, j, 0)
    per_b = lambda i, j: (i, 0, 0)
    const = lambda i, j: (0, 0)
    sc1, sh1, gt1, sc2, sh2 = mods
    return pl.pallas_call(
        functools.partial(_merge_kernel, d=d),
        grid=(b, s // tm),
        in_specs=[pl.BlockSpec((1, tm, d), row),
                  pl.BlockSpec((1, tm, pool_dim), row),
                  pl.BlockSpec((1, tm, d), row)]
                 + [pl.BlockSpec((1, 1, d), per_b)] * 5
                 + [pl.BlockSpec((1, d), const), pl.BlockSpec((1, d), const),
                    pl.BlockSpec((pool_dim, d), const), pl.BlockSpec((d, d), const),
                    pl.BlockSpec((d, 2 * d), const), pl.BlockSpec((1, 2 * d), const),
                    pl.BlockSpec((d, d), const)],
        out_specs=[pl.BlockSpec((1, tm, d), row), pl.BlockSpec((1, tm, d), row)],
        out_shape=[jax.ShapeDtypeStruct((b, s, d), F32), jax.ShapeDtypeStruct((b, s, d), BF16)],
        compiler_params=_cparams(("parallel", "parallel")),
        name="merge",
    )(x, mixed, att, sc1, sh1, gt1, sc2, sh2, g1, g2, wpool, watt, wgate, bgate, wout)


def _route_kernel(h2_ref, wqt_ref, keys_ref, e1_ref, e2_ref, gate_ref,
                  qt_scr, tv_scr, ti_scr, cs_scr, c1_scr, c2_scr, bs_scr, *, tt):
    qt = lax.dot_general(wqt_ref[...], h2_ref[...], (((1,), (1,)), ((), ())),
                         preferred_element_type=F32)
    qt_scr[...] = qt.astype(BF16)
    iota_k = lax.broadcasted_iota(I32, (LANES, tt), 0)
    iota_c = lax.broadcasted_iota(I32, (N_CAND_PAD, tt), 0)
    cs_scr[N_CAND:N_CAND_PAD, :] = jnp.full((N_CAND_PAD - N_CAND, tt), NEG_INF, F32)
    c1_scr[N_CAND:N_CAND_PAD, :] = jnp.zeros((N_CAND_PAD - N_CAND, tt), I32)
    c2_scr[N_CAND:N_CAND_PAD, :] = jnp.zeros((N_CAND_PAD - N_CAND, tt), I32)

    def head_body(h, carry):
        for part in range(2):
            r0 = pl.multiple_of((2 * h + part) * HEAD_DIM, HEAD_DIM)
            s = jnp.dot(keys_ref[h, part], qt_scr[pl.ds(r0, HEAD_DIM), :],
                        preferred_element_type=F32)

            def top_body(kk, s):
                m = jnp.max(s, axis=0, keepdims=True)
                idx = jnp.min(jnp.where(s == m, iota_k, LANES), axis=0, keepdims=True)
                tv_scr[pl.ds(part * TOPK + kk, 1), :] = m
                ti_scr[pl.ds(part * TOPK + kk, 1), :] = idx
                return jnp.where(iota_k == idx, NEG_INF, s)

            lax.fori_loop(0, TOPK, top_body, s)
        off = 0
        for a, nb in CAND_ROWS:
            cs_scr[off:off + nb, :] = tv_scr[a:a + 1, :] + tv_scr[TOPK:TOPK + nb, :]
            c1_scr[off:off + nb, :] = jnp.broadcast_to(ti_scr[a:a + 1, :], (nb, tt))
            c2_scr[off:off + nb, :] = ti_scr[TOPK:TOPK + nb, :]
            off += nb

        def best_body(kk, cand):
            m = jnp.max(cand, axis=0, keepdims=True)
            pos = jnp.min(jnp.where(cand == m, iota_c, N_CAND_PAD), axis=0, keepdims=True)
            hit = iota_c == pos
            row = pl.ds(h * TOPK + kk, 1)
            bs_scr[pl.ds(kk, 1), :] = m
            e1_ref[row, :] = jnp.max(jnp.where(hit, c1_scr[...], -1), axis=0, keepdims=True)
            e2_ref[row, :] = jnp.max(jnp.where(hit, c2_scr[...], -1), axis=0, keepdims=True)
            return jnp.where(hit, NEG_INF, cand)

        lax.fori_loop(0, TOPK, best_body, cs_scr[...])
        bs = bs_scr[...]
        ex = jnp.exp(bs - bs[0:1, :])
        gate_ref[pl.ds(pl.multiple_of(h * TOPK, TOPK), TOPK), :] = (
            ex / jnp.sum(ex, axis=0, keepdims=True))
        return carry

    lax.fori_loop(0, N_HEADS, head_body, 0)


def _route(h2_flat, wqt_bf, keys_bf, tt):
    t, d = h2_flat.shape
    nslot = N_HEADS * TOPK
    out = pl.BlockSpec((nslot, tt), lambda i: (0, i))
    return pl.pallas_call(
        functools.partial(_route_kernel, tt=tt),
        grid=(t // tt,),
        in_specs=[pl.BlockSpec((tt, d), lambda i: (i, 0)),
                  pl.BlockSpec(wqt_bf.shape, lambda i: (0, 0)),
                  pl.BlockSpec(keys_bf.shape, lambda i: (0, 0, 0, 0))],
        out_specs=[out, out, out],
        out_shape=[jax.ShapeDtypeStruct((nslot, t), I32),
                   jax.ShapeDtypeStruct((nslot, t), I32),
                   jax.ShapeDtypeStruct((nslot, t), F32)],
        scratch_shapes=[pltpu.VMEM((wqt_bf.shape[0], tt), BF16),
                        pltpu.VMEM((2 * TOPK, tt), F32),
                        pltpu.VMEM((2 * TOPK, tt), I32),
                        pltpu.VMEM((N_CAND_PAD, tt), F32),
                        pltpu.VMEM((N_CAND_PAD, tt), I32),
                        pltpu.VMEM((N_CAND_PAD, tt), I32),
                        pltpu.VMEM((TOPK, tt), F32)],
        compiler_params=_cparams(("parallel",)),
        name="route",
    )(h2_flat, wqt_bf, keys_bf)


GATE_PITCH = LANES + 4


def _peer_kernel(h2_ref, e1_ref, e2_ref, gate_ref, ut_ref, v_ref, x1_ref, gt2_ref, gf_ref,
                 y_ref, gmat_scr, w_scr, acc_scr, *, tp, ec):
    c = pl.program_id(1)
    nsub = ec // LANES

    @pl.when(c == 0)
    def _build_gate_matrices():
        acc_scr[...] = jnp.zeros_like(acc_scr)
        key_id = lax.broadcasted_iota(I32, (LANES, LANES), 0)

        def body(t, carry):
            r1 = e1_ref[pl.ds(t, 1), :]
            r2 = e2_ref[pl.ds(t, 1), :]
            gg = gate_ref[pl.ds(t, 1), :]
            a = jnp.where(key_id == r1, gg, 0.0).astype(BF16)
            b = jnp.where(key_id == r2, 1.0, 0.0).astype(BF16)
            gm = lax.dot_general(a, b, (((1,), (1,)), ((), ())), preferred_element_type=F32)
            gmat_scr[pl.ds(t * GATE_PITCH, LANES), :] = gm
            return carry

        lax.fori_loop(0, tp, body, 0)

    s = jnp.dot(h2_ref[...], ut_ref[...], preferred_element_type=F32)
    for j in range(nsub):
        gi = gmat_scr[pl.ds(c * nsub + j, tp, stride=GATE_PITCH), :]
        sj = s[:, j * LANES:(j + 1) * LANES]
        w = (0.5 * sj) * (1.0 + lax.erf(sj * INV_SQRT2)) * gi
        w_scr[:, j * LANES:(j + 1) * LANES] = w.astype(BF16)
    acc_scr[...] += jnp.dot(w_scr[...], v_ref[...], preferred_element_type=F32)

    @pl.when(c == pl.num_programs(1) - 1)
    def _finish():
        xo = x1_ref[...] + gt2_ref[0] * acc_scr[...]
        y_ref[...] = _rms(xo) * gf_ref[...]


def _peer(h2_flat, e1, e2, gate, ut_bf, v_bf, x1_flat, gt2, g_final, tokens_per_batch, tp, ec):
    t, d = h2_flat.shape
    ne = v_bf.shape[0]
    nslot = e1.shape[1]
    tiles_per_batch = tokens_per_batch // tp
    tok = lambda i, c: (i, 0)
    return pl.pallas_call(
        functools.partial(_peer_kernel, tp=tp, ec=ec),
        grid=(t // tp, ne // ec),
        in_specs=[pl.BlockSpec((tp, d), tok),
                  pl.BlockSpec((tp, nslot), tok),
                  pl.BlockSpec((tp, nslot), tok),
                  pl.BlockSpec((tp, nslot), tok),
                  pl.BlockSpec((d, ec), lambda i, c: (0, c)),
                  pl.BlockSpec((ec, d), lambda i, c: (c, 0)),
                  pl.BlockSpec((tp, d), tok),
                  pl.BlockSpec((1, 1, d), lambda i, c: (i // tiles_per_batch, 0, 0)),
                  pl.BlockSpec((1, d), lambda i, c: (0, 0))],
        out_specs=pl.BlockSpec((tp, d), tok),
        out_shape=jax.ShapeDtypeStruct((t, d), F32),
        scratch_shapes=[pltpu.VMEM((tp * GATE_PITCH, LANES), F32),
                        pltpu.VMEM((tp, ec), BF16),
                        pltpu.VMEM((tp, d), F32)],
        compiler_params=_cparams(("parallel", "arbitrary")),
        name="peer",
    )(h2_flat, e1, e2, gate, ut_bf, v_bf, x1_flat, gt2, g_final)


def _rope_tables(s):
    inv_freq = 1.0 / (ROPE_THETA ** (jnp.arange(0, HEAD_DIM, 2, dtype=F32) / HEAD_DIM))
    ang = jnp.arange(s, dtype=F32)[:, None] * inv_freq[None, :]
    ang = jnp.concatenate([ang, ang, ang, ang], axis=-1)
    cos, sin = jnp.cos(ang), jnp.sin(ang)
    low = (jnp.arange(LANES) % HEAD_DIM) < (HEAD_DIM // 2)
    return cos, jnp.where(low, -sin, 0.0), jnp.where(low, 0.0, sin)


def _pick_tile(n, want):
    t = min(n, want)
    while n % t:
        t //= 2
    return t


def _trunk(x, mod, wts, lam_init):
    b, s, d = x.shape
    sh1, sc1, gt1, sh2, sc2, gt2 = [m.reshape(b, 1, d) for m in jnp.split(mod, 6, axis=-1)]
    cos, sa, sb = _rope_tables(s)
    p, q, k, v = _front(x, sc1, sh1, wts["g_norm1"], wts["w_in"], cos, sa, sb, _pick_tile(s, 512))
    mixed = _pool(p, wts["pool_w"], wts["pool_scale"])
    att = _attention(q, k, v, wts["lamv"], wts["g_sub"], lam_init, _pick_tile(s, 256))
    x1, h2 = _merge(x, mixed, att, (sc1, sh1, gt1, sc2, sh2), wts["g_norm1"], wts["g_norm2"],
                    wts["w_pool_out"], wts["w_att_out"], wts["w_gate"], wts["b_gate"],
                    wts["w_out"], _pick_tile(s, 512))
    t = b * s
    h2f = h2.reshape(t, d)
    e1t, e2t, gatet = _route(h2f, wts["w_peer_qt"], wts["peer_keys"], _pick_tile(t, 256))
    y = _peer(h2f, e1t.T, e2t.T, gatet.T, wts["peer_ut"], wts["peer_v"], x1.reshape(t, d),
              gt2, wts["g_final"], s, _pick_tile(s, 256), 1024)
    return y.reshape(b, s, d)


def kernel(x_prompt, x_sample, c_prompt, c_sample, w_ada, b_ada, g_norm1, w_in, pool_w, pool_scale,
           w_pool_out, lam_q1, lam_k1, lam_q2, lam_k2, g_sub, w_att_out, w_gate, b_gate, w_out,
           g_norm2, w_peer_q, peer_keys, peer_u, peer_v, g_final):
    depth = w_ada.shape[0]
    assert depth == 1, "single-layer trunk"
    l = 0
    lam_init = 0.8 - 0.6 * math.exp(-0.3 * l)
    d = x_prompt.shape[-1]
    wts = {
        "g_norm1": g_norm1[l][None, :], "g_norm2": g_norm2[l][None, :], "g_final": g_final[None, :],
        "w_in": w_in[l].astype(BF16),
        "pool_w": pool_w[l].astype(BF16), "pool_scale": pool_scale[l][None, :],
        "w_pool_out": w_pool_out[l].astype(BF16),
        "lamv": jnp.stack([lam_q1[l], lam_k1[l], lam_q2[l], lam_k2[l]]).astype(F32),
        "g_sub": g_sub[l][None, :],
        "w_att_out": w_att_out[l].astype(BF16),
        "w_gate": w_gate[l].astype(BF16), "b_gate": b_gate[l][None, :],
        "w_out": w_out[l].astype(BF16),
        "w_peer_qt": w_peer_q[l].T.astype(BF16),
        "peer_keys": peer_keys[l].astype(BF16),
        "peer_ut": peer_u[l].T.astype(BF16),
        "peer_v": peer_v[l].astype(BF16),
    }
    nb = x_prompt.shape[0]
    mod = _ada(jnp.concatenate([c_prompt, c_sample], axis=0), w_ada[l].astype(BF16), b_ada[l][None, :])
    y_prompt = _trunk(x_prompt, mod[:nb], wts, lam_init)
    y_sample = _trunk(x_sample, mod[nb:], wts, lam_init)
    return (y_prompt, y_sample)
```

```python
import functools
import math

import jax
import jax.numpy as jnp
from jax import lax
from jax.experimental import pallas as pl
from jax.experimental.pallas import tpu as pltpu

F32 = jnp.float32
BF16 = jnp.bfloat16
I32 = jnp.int32

NORM_EPS = 1e-6
ROPE_THETA = 10000.0
POOL_WINDOWS = (2, 4, 8, 16)
POOL_PAD = 16
N_HEADS = 8
HEAD_DIM = 64
LANES = 128
TOPK = 16
CAND_ROWS = tuple((a, TOPK // (a + 1)) for a in range(TOPK))
N_CAND = sum(n for _, n in CAND_ROWS)
N_CAND_PAD = 56
VMEM_LIMIT = 56 * 1024 * 1024
INV_SQRT2 = 1.0 / math.sqrt(2.0)
NEG_INF = float("-inf")


def _cparams(sem):
    return pltpu.CompilerParams(dimension_semantics=sem, vmem_limit_bytes=VMEM_LIMIT)


def _rms(x):
    return x * lax.rsqrt(jnp.mean(x * x, axis=-1, keepdims=True) + NORM_EPS)


def _ada_kernel(c_ref, w_ref, b_ref, o_ref):
    c = c_ref[...]
    a = (c * jax.nn.sigmoid(c)).astype(BF16)
    o_ref[...] = jnp.dot(a, w_ref[...], preferred_element_type=F32) + b_ref[...]


def _ada(c, w_bf, b):
    n, d = c.shape
    nout = w_bf.shape[1]
    return pl.pallas_call(
        _ada_kernel,
        grid=(nout // d,),
        in_specs=[pl.BlockSpec((n, d), lambda j: (0, 0)),
                  pl.BlockSpec((d, d), lambda j: (0, j)),
                  pl.BlockSpec((1, d), lambda j: (0, j))],
        out_specs=pl.BlockSpec((n, d), lambda j: (0, j)),
        out_shape=jax.ShapeDtypeStruct((n, nout), F32),
        compiler_params=_cparams(("parallel",)),
        name="ada",
    )(c, w_bf, b)


def _front_kernel(x_ref, sc_ref, sh_ref, g_ref, w_ref, cos_ref, sa_ref, sb_ref,
                  p_ref, q_ref, k_ref, v_ref, *, d, pool_dim):
    x = x_ref[0]
    h = (_rms(x) * g_ref[...]) * (1.0 + sc_ref[0]) + sh_ref[0]
    proj = jnp.dot(h.astype(BF16), w_ref[...], preferred_element_type=F32)
    p_ref[0] = proj[:, :pool_dim]
    cos, sa, sb = cos_ref[...], sa_ref[...], sb_ref[...]
    half = HEAD_DIM // 2

    def rope(t):
        return (t * cos + pltpu.roll(t, LANES - half, axis=1) * sa
                + pltpu.roll(t, half, axis=1) * sb)

    qscale = HEAD_DIM ** -0.5
    for j in range(d // LANES):
        o = pool_dim + j * LANES
        q_ref[0, :, j * LANES:(j + 1) * LANES] = (rope(proj[:, o:o + LANES]) * qscale).astype(BF16)
        o = pool_dim + d + j * LANES
        k_ref[0, :, j * LANES:(j + 1) * LANES] = rope(proj[:, o:o + LANES]).astype(BF16)
    v_ref[0] = proj[:, pool_dim + 2 * d:].astype(BF16)


def _front(x, sc, sh, g, w_in_bf, cos, sa, sb, tm):
    b, s, d = x.shape
    in_dim = w_in_bf.shape[1]
    pool_dim = in_dim - 3 * d
    row = lambda i, j: (i, j, 0)
    per_b = lambda i, j: (i, 0, 0)
    const = lambda i, j: (0, 0)
    return pl.pallas_call(
        functools.partial(_front_kernel, d=d, pool_dim=pool_dim),
        grid=(b, s // tm),
        in_specs=[pl.BlockSpec((1, tm, d), row),
                  pl.BlockSpec((1, 1, d), per_b),
                  pl.BlockSpec((1, 1, d), per_b),
                  pl.BlockSpec((1, d), const),
                  pl.BlockSpec((d, in_dim), const),
                  pl.BlockSpec((tm, LANES), lambda i, j: (j, 0)),
                  pl.BlockSpec((tm, LANES), lambda i, j: (j, 0)),
                  pl.BlockSpec((tm, LANES), lambda i, j: (j, 0))],
        out_specs=[pl.BlockSpec((1, tm, pool_dim), row),
                   pl.BlockSpec((1, tm, d), row),
                   pl.BlockSpec((1, tm, d), row),
                   pl.BlockSpec((1, tm, d), row)],
        out_shape=[jax.ShapeDtypeStruct((b, s, pool_dim), F32),
                   jax.ShapeDtypeStruct((b, s, d), BF16),
                   jax.ShapeDtypeStruct((b, s, d), BF16),
                   jax.ShapeDtypeStruct((b, s, d), BF16)],
        compiler_params=_cparams(("parallel", "parallel")),
        name="front",
    )(x, sc, sh, g, w_in_bf, cos, sa, sb)


def _pool_kernel(p_ref, w_ref, scale_ref, o_ref, pad_ref, *, s):
    pos = lax.broadcasted_iota(I32, (s, LANES), 0)
    zeros = jnp.zeros((POOL_PAD, LANES), F32)
    for g, win in enumerate(POOL_WINDOWS):
        half = win // 2
        xg = p_ref[0, :, g * LANES:(g + 1) * LANES]
        pad_ref[0:POOL_PAD, :] = zeros
        pad_ref[POOL_PAD:POOL_PAD + s, :] = xg
        pad_ref[POOL_PAD + s:2 * POOL_PAD + s, :] = zeros
        wsum = pad_ref[POOL_PAD - half:POOL_PAD - half + s, :]
        for dlt in range(-half + 1, half):
            wsum = wsum + pad_ref[POOL_PAD + dlt:POOL_PAD + dlt + s, :]
        cnt = (jnp.minimum(pos + half, s) - jnp.maximum(pos - half, 0)).astype(F32)
        diff = wsum / cnt - xg
        mixed = jnp.dot(diff.astype(BF16), w_ref[g], preferred_element_type=F32)
        o_ref[0, :, g * LANES:(g + 1) * LANES] = (
            mixed * scale_ref[:, g * LANES:(g + 1) * LANES]).astype(BF16)


def _pool(p, pool_w_bf, pool_scale):
    b, s, pool_dim = p.shape
    ng = pool_w_bf.shape[0]
    return pl.pallas_call(
        functools.partial(_pool_kernel, s=s),
        grid=(b,),
        in_specs=[pl.BlockSpec((1, s, pool_dim), lambda i: (i, 0, 0)),
                  pl.BlockSpec((ng, LANES, LANES), lambda i: (0, 0, 0)),
                  pl.BlockSpec((1, pool_dim), lambda i: (0, 0))],
        out_specs=pl.BlockSpec((1, s, pool_dim), lambda i: (i, 0, 0)),
        out_shape=jax.ShapeDtypeStruct((b, s, pool_dim), BF16),
        scratch_shapes=[pltpu.VMEM((s + 2 * POOL_PAD, LANES), F32)],
        compiler_params=_cparams(("parallel",)),
        name="pool",
    )(p, pool_w_bf, pool_scale)


def _attn_kernel(lamv_ref, q_ref, k_ref, v_ref, gsub_ref, o_ref, *, lam_init):
    lv = lamv_ref[...]
    lam = (jnp.exp(jnp.sum(lv[0:1] * lv[1:2], axis=-1, keepdims=True))
           - jnp.exp(jnp.sum(lv[2:3] * lv[3:4], axis=-1, keepdims=True)) + lam_init)
    q = q_ref[0]
    k = k_ref[0]
    first = lax.broadcasted_iota(I32, (1, LANES), 1) < HEAD_DIM
    zero = jnp.zeros_like(q)
    nt = (((1,), (1,)), ((), ()))
    s1 = lax.dot_general(jnp.where(first, q, zero), k, nt, preferred_element_type=F32)
    s2 = lax.dot_general(jnp.where(first, zero, q), k, nt, preferred_element_type=F32)
    p1 = jnp.exp(s1 - jnp.max(s1, axis=-1, keepdims=True))
    p2 = jnp.exp(s2 - jnp.max(s2, axis=-1, keepdims=True))
    c1 = 1.0 / jnp.sum(p1, axis=-1, keepdims=True)
    c2 = lam / jnp.sum(p2, axis=-1, keepdims=True)
    a = p1 * c1 - p2 * c2
    o = jnp.dot(a.astype(BF16), v_ref[0], preferred_element_type=F32)
    o_ref[0] = ((_rms(o) * gsub_ref[...]) * (1.0 - lam_init)).astype(BF16)


def _attention(q, k, v, lamv, g_sub, lam_init, tq):
    b, s, d = q.shape
    nh = d // LANES
    return pl.pallas_call(
        functools.partial(_attn_kernel, lam_init=lam_init),
        grid=(b, nh, s // tq),
        in_specs=[pl.BlockSpec((4, HEAD_DIM), lambda i, h, j: (0, 0)),
                  pl.BlockSpec((1, tq, LANES), lambda i, h, j: (i, j, h)),
                  pl.BlockSpec((1, s, LANES), lambda i, h, j: (i, 0, h)),
                  pl.BlockSpec((1, s, LANES), lambda i, h, j: (i, 0, h)),
                  pl.BlockSpec((1, LANES), lambda i, h, j: (0, 0))],
        out_specs=pl.BlockSpec((1, tq, LANES), lambda i, h, j: (i, j, h)),
        out_shape=jax.ShapeDtypeStruct((b, s, d), BF16),
        compiler_params=_cparams(("parallel", "parallel", "arbitrary")),
        name="attn",
    )(lamv, q, k, v, g_sub)


def _merge_kernel(x_ref, mixed_ref, att_ref, sc1_ref, sh1_ref, gt1_ref, sc2_ref, sh2_ref,
                  g1_ref, g2_ref, wpool_ref, watt_ref, wgate_ref, bgate_ref, wout_ref,
                  x1_ref, h2_ref, *, d):
    x = x_ref[0]
    h = ((_rms(x) * g1_ref[...]) * (1.0 + sc1_ref[0]) + sh1_ref[0]).astype(BF16)
    gates = jax.nn.sigmoid(jnp.dot(h, wgate_ref[...], preferred_element_type=F32) + bgate_ref[...])
    y_pool = jnp.dot(mixed_ref[0], wpool_ref[...], preferred_element_type=F32)
    y_att = jnp.dot(att_ref[0], watt_ref[...], preferred_element_type=F32)
    merged = gates[:, :d] * y_pool + gates[:, d:] * y_att
    x1 = x + gt1_ref[0] * jnp.dot(merged.astype(BF16), wout_ref[...], preferred_element_type=F32)
    x1_ref[0] = x1
    h2 = (_rms(x1) * g2_ref[...]) * (1.0 + sc2_ref[0]) + sh2_ref[0]
    h2_ref[0] = h2.astype(BF16)


def _merge(x, mixed, att, mods, g1, g2, wpool, watt, wgate, bgate, wout, tm):
    b, s, d = x.shape
    pool_dim = mixed.shape[-1]
    row = lambda i, j: (i, j, 0)
    per_b = lambda i, j: (i, 0, 0)
    const = lambda i, j: (0, 0)
    sc1, sh1, gt1, sc2, sh2 = mods
    return pl.pallas_call(
        functools.partial(_merge_kernel, d=d),
        grid=(b, s // tm),
        in_specs=[pl.BlockSpec((1, tm, d), row),
                  pl.BlockSpec((1, tm, pool_dim), row),
                  pl.BlockSpec((1, tm, d), row)]
                 + [pl.BlockSpec((1, 1, d), per_b)] * 5
                 + [pl.BlockSpec((1, d), const), pl.BlockSpec((1, d), const),
                    pl.BlockSpec((pool_dim, d), const), pl.BlockSpec((d, d), const),
                    pl.BlockSpec((d, 2 * d), const), pl.BlockSpec((1, 2 * d), const),
                    pl.BlockSpec((d, d), const)],
        out_specs=[pl.BlockSpec((1, tm, d), row), pl.BlockSpec((1, tm, d), row)],
        out_shape=[jax.ShapeDtypeStruct((b, s, d), F32), jax.ShapeDtypeStruct((b, s, d), BF16)],
        compiler_params=_cparams(("parallel", "parallel")),
        name="merge",
    )(x, mixed, att, sc1, sh1, gt1, sc2, sh2, g1, g2, wpool, watt, wgate, bgate, wout)


def _route_kernel(h2_ref, wqt_ref, keys_ref, e1_ref, e2_ref, gate_ref,
                  qt_scr, tv_scr, ti_scr, cs_scr, c1_scr, c2_scr, bs_scr, *, tt):
    qt = lax.dot_general(wqt_ref[...], h2_ref[...], (((1,), (1,)), ((), ())),
                         preferred_element_type=F32)
    qt_scr[...] = qt.astype(BF16)
    iota_k = lax.broadcasted_iota(I32, (LANES, tt), 0)
    iota_c = lax.broadcasted_iota(I32, (N_CAND_PAD, tt), 0)
    cs_scr[N_CAND:N_CAND_PAD, :] = jnp.full((N_CAND_PAD - N_CAND, tt), NEG_INF, F32)
    c1_scr[N_CAND:N_CAND_PAD, :] = jnp.zeros((N_CAND_PAD - N_CAND, tt), I32)
    c2_scr[N_CAND:N_CAND_PAD, :] = jnp.zeros((N_CAND_PAD - N_CAND, tt), I32)

    def head_body(h, carry):
        for part in range(2):
            r0 = pl.multiple_of((2 * h + part) * HEAD_DIM, HEAD_DIM)
            s = jnp.dot(keys_ref[h, part], qt_scr[pl.ds(r0, HEAD_DIM), :],
                        preferred_element_type=F32)

            def top_body(kk, s):
                m = jnp.max(s, axis=0, keepdims=True)
                idx = jnp.min(jnp.where(s == m, iota_k, LANES), axis=0, keepdims=True)
                tv_scr[pl.ds(part * TOPK + kk, 1), :] = m
                ti_scr[pl.ds(part * TOPK + kk, 1), :] = idx
                return jnp.where(iota_k == idx, NEG_INF, s)

            lax.fori_loop(0, TOPK, top_body, s)
        off = 0
        for a, nb in CAND_ROWS:
            cs_scr[off:off + nb, :] = tv_scr[a:a + 1, :] + tv_scr[TOPK:TOPK + nb, :]
            c1_scr[off:off + nb, :] = jnp.broadcast_to(ti_scr[a:a + 1, :], (nb, tt))
            c2_scr[off:off + nb, :] = ti_scr[TOPK:TOPK + nb, :]
            off += nb

        def best_body(kk, cand):
            m = jnp.max(cand, axis=0, keepdims=True)
            pos = jnp.min(jnp.where(cand == m, iota_c, N_CAND_PAD), axis=0, keepdims=True)
            hit = iota_c == pos
            row = pl.ds(h * TOPK + kk, 1)
            bs_scr[pl.ds(kk, 1), :] = m
            e1_ref[row, :] = jnp.max(jnp.where(hit, c1_scr[...], -1), axis=0, keepdims=True)
            e2_ref[row, :] = jnp.max(jnp.where(hit, c2_scr[...], -1), axis=0, keepdims=True)
            return jnp.where(hit, NEG_INF, cand)

        lax.fori_loop(0, TOPK, best_body, cs_scr[...])
        bs = bs_scr[...]
        ex = jnp.exp(bs - bs[0:1, :])
        gate_ref[pl.ds(pl.multiple_of(h * TOPK, TOPK), TOPK), :] = (
            ex / jnp.sum(ex, axis=0, keepdims=True))
        return carry

    lax.fori_loop(0, N_HEADS, head_body, 0)


def _route(h2_flat, wqt_bf, keys_bf, tt):
    t, d = h2_flat.shape
    nslot = N_HEADS * TOPK
    out = pl.BlockSpec((nslot, tt), lambda i: (0, i))
    return pl.pallas_call(
        functools.partial(_route_kernel, tt=tt),
        grid=(t // tt,),
        in_specs=[pl.BlockSpec((tt, d), lambda i: (i, 0)),
                  pl.BlockSpec(wqt_bf.shape, lambda i: (0, 0)),
                  pl.BlockSpec(keys_bf.shape, lambda i: (0, 0, 0, 0))],
        out_specs=[out, out, out],
        out_shape=[jax.ShapeDtypeStruct((nslot, t), I32),
                   jax.ShapeDtypeStruct((nslot, t), I32),
                   jax.ShapeDtypeStruct((nslot, t), F32)],
        scratch_shapes=[pltpu.VMEM((wqt_bf.shape[0], tt), BF16),
                        pltpu.VMEM((2 * TOPK, tt), F32),
                        pltpu.VMEM((2 * TOPK, tt), I32),
                        pltpu.VMEM((N_CAND_PAD, tt), F32),
                        pltpu.VMEM((N_CAND_PAD, tt), I32),
                        pltpu.VMEM((N_CAND_PAD, tt), I32),
                        pltpu.VMEM((TOPK, tt), F32)],
        compiler_params=_cparams(("parallel",)),
        name="route",
    )(h2_flat, wqt_bf, keys_bf)


GATE_ROWS = LANES // 2
GATE_PITCH = GATE_ROWS + 4
BUILD_GROUP = 8
HI_MASK = -65536


def _peer_kernel(h2_ref, e1_ref, e2_ref, gate_ref, ut_ref, v_ref, x1_ref, gt2_ref, gf_ref,
                 y_ref, gmat_scr, w_scr, acc_scr, *, tp, ec):
    c = pl.program_id(1)
    nsub = ec // LANES

    @pl.when(c == 0)
    def _build_gate_matrices():
        acc_scr[...] = jnp.zeros_like(acc_scr)
        key_id = lax.broadcasted_iota(I32, (LANES, LANES), 0)

        def body(tb, carry):
            base = pl.multiple_of(tb * BUILD_GROUP, BUILD_GROUP)
            r1 = e1_ref[pl.ds(base, BUILD_GROUP), :]
            r2 = e2_ref[pl.ds(base, BUILD_GROUP), :]
            gg = gate_ref[pl.ds(base, BUILD_GROUP), :]
            for i in range(BUILD_GROUP):
                a = jnp.where(key_id == r1[i:i + 1, :], gg[i:i + 1, :], 0.0).astype(BF16)
                b = jnp.where(key_id == r2[i:i + 1, :], 1.0, 0.0).astype(BF16)
                gm = lax.dot_general(a, b, (((1,), (1,)), ((), ())), preferred_element_type=F32)
                bits = lax.bitcast_convert_type(gm, I32) + 32768
                packed = (bits[:GATE_ROWS] & HI_MASK) | lax.shift_right_logical(bits[GATE_ROWS:], 16)
                gmat_scr[pl.ds((base + i) * GATE_PITCH, GATE_ROWS), :] = packed
            return carry

        lax.fori_loop(0, tp // BUILD_GROUP, body, 0)

    s = jnp.dot(h2_ref[...], ut_ref[...], preferred_element_type=F32)
    for j in range(nsub):
        k1 = c * nsub + j
        word = gmat_scr[pl.ds(k1 % GATE_ROWS, tp, stride=GATE_PITCH), :]
        shift = (k1 // GATE_ROWS) * 16
        gi = lax.bitcast_convert_type(lax.shift_left(word, shift) & HI_MASK, F32)
        sj = s[:, j * LANES:(j + 1) * LANES]
        w = (0.5 * sj) * (1.0 + lax.erf(sj * INV_SQRT2)) * gi
        w_scr[:, j * LANES:(j + 1) * LANES] = w.astype(BF16)
    acc_scr[...] += jnp.dot(w_scr[...], v_ref[...], preferred_element_type=F32)

    @pl.when(c == pl.num_programs(1) - 1)
    def _finish():
        xo = x1_ref[...] + gt2_ref[0] * acc_scr[...]
        y_ref[...] = _rms(xo) * gf_ref[...]


def _peer(h2_flat, e1, e2, gate, ut_bf, v_bf, x1_flat, gt2, g_final, tokens_per_batch, tp, ec):
    t, d = h2_flat.shape
    ne = v_bf.shape[0]
    nslot = e1.shape[1]
    tiles_per_batch = tokens_per_batch // tp
    tok = lambda i, c: (i, 0)
    return pl.pallas_call(
        functools.partial(_peer_kernel, tp=tp, ec=ec),
        grid=(t // tp, ne // ec),
        in_specs=[pl.BlockSpec((tp, d), tok),
                  pl.BlockSpec((tp, nslot), tok),
                  pl.BlockSpec((tp, nslot), tok),
                  pl.BlockSpec((tp, nslot), tok),
                  pl.BlockSpec((d, ec), lambda i, c: (0, c)),
                  pl.BlockSpec((ec, d), lambda i, c: (c, 0)),
                  pl.BlockSpec((tp, d), tok),
                  pl.BlockSpec((1, 1, d), lambda i, c: (i // tiles_per_batch, 0, 0)),
                  pl.BlockSpec((1, d), lambda i, c: (0, 0))],
        out_specs=pl.BlockSpec((tp, d), tok),
        out_shape=jax.ShapeDtypeStruct((t, d), F32),
        scratch_shapes=[pltpu.VMEM((tp * GATE_PITCH, LANES), I32),
                        pltpu.VMEM((tp, ec), BF16),
                        pltpu.VMEM((tp, d), F32)],
        compiler_params=_cparams(("parallel", "arbitrary")),
        name="peer",
    )(h2_flat, e1, e2, gate, ut_bf, v_bf, x1_flat, gt2, g_final)


def _rope_tables(s):
    inv_freq = 1.0 / (ROPE_THETA ** (jnp.arange(0, HEAD_DIM, 2, dtype=F32) / HEAD_DIM))
    ang = jnp.arange(s, dtype=F32)[:, None] * inv_freq[None, :]
    ang = jnp.concatenate([ang, ang, ang, ang], axis=-1)
    cos, sin = jnp.cos(ang), jnp.sin(ang)
    low = (jnp.arange(LANES) % HEAD_DIM) < (HEAD_DIM // 2)
    return cos, jnp.where(low, -sin, 0.0), jnp.where(low, 0.0, sin)


def _pick_tile(n, want):
    t = min(n, want)
    while n % t:
        t //= 2
    return t


def _trunk(x, mod, wts, lam_init):
    b, s, d = x.shape
    sh1, sc1, gt1, sh2, sc2, gt2 = [m.reshape(b, 1, d) for m in jnp.split(mod, 6, axis=-1)]
    cos, sa, sb = _rope_tables(s)
    p, q, k, v = _front(x, sc1, sh1, wts["g_norm1"], wts["w_in"], cos, sa, sb, _pick_tile(s, 512))
    mixed = _pool(p, wts["pool_w"], wts["pool_scale"])
    att = _attention(q, k, v, wts["lamv"], wts["g_sub"], lam_init, _pick_tile(s, 256))
    x1, h2 = _merge(x, mixed, att, (sc1, sh1, gt1, sc2, sh2), wts["g_norm1"], wts["g_norm2"],
                    wts["w_pool_out"], wts["w_att_out"], wts["w_gate"], wts["b_gate"],
                    wts["w_out"], _pick_tile(s, 512))
    t = b * s
    h2f = h2.reshape(t, d)
    e1t, e2t, gatet = _route(h2f, wts["w_peer_qt"], wts["peer_keys"], _pick_tile(t, 256))
    y = _peer(h2f, e1t.T, e2t.T, gatet.T, wts["peer_ut"], wts["peer_v"], x1.reshape(t, d),
              gt2, wts["g_final"], s, _pick_tile(s, 512), 1024)
    return y.reshape(b, s, d)


def kernel(x_prompt, x_sample, c_prompt, c_sample, w_ada, b_ada, g_norm1, w_in, pool_w, pool_scale,
           w_pool_out, lam_q1, lam_k1, lam_q2, lam_k2, g_sub, w_att_out, w_gate, b_gate, w_out,
           g_norm2, w_peer_q, peer_keys, peer_u, peer_v, g_final):
    depth = w_ada.shape[0]
    assert depth == 1, "single-layer trunk"
    l = 0
    lam_init = 0.8 - 0.6 * math.exp(-0.3 * l)
    d = x_prompt.shape[-1]
    wts = {
        "g_norm1": g_norm1[l][None, :], "g_norm2": g_norm2[l][None, :], "g_final": g_final[None, :],
        "w_in": w_in[l].astype(BF16),
        "pool_w": pool_w[l].astype(BF16), "pool_scale": pool_scale[l][None, :],
        "w_pool_out": w_pool_out[l].astype(BF16),
        "lamv": jnp.stack([lam_q1[l], lam_k1[l], lam_q2[l], lam_k2[l]]).astype(F32),
        "g_sub": g_sub[l][None, :],
        "w_att_out": w_att_out[l].astype(BF16),
        "w_gate": w_gate[l].astype(BF16), "b_gate": b_gate[l][None, :],
        "w_out": w_out[l].astype(BF16),
        "w_peer_qt": w_peer_q[l].T.astype(BF16),
        "peer_keys": peer_keys[l].astype(BF16),
        "peer_ut": peer_u[l].T.astype(BF16),
        "peer_v": peer_v[l].astype(BF16),
    }
    nb = x_prompt.shape[0]
    mod = _ada(jnp.concatenate([c_prompt, c_sample], axis=0), w_ada[l].astype(BF16), b_ada[l][None, :])
    y_prompt = _trunk(x_prompt, mod[:nb], wts, lam_init)
    y_sample = _trunk(x_sample, mod[nb:], wts, lam_init)
    return (y_prompt, y_sample)
```

```python
import functools
import math

import jax
import jax.numpy as jnp
from jax import lax
from jax.experimental import pallas as pl
from jax.experimental.pallas import tpu as pltpu

F32 = jnp.float32
BF16 = jnp.bfloat16
I32 = jnp.int32

NORM_EPS = 1e-6
ROPE_THETA = 10000.0
POOL_WINDOWS = (2, 4, 8, 16)
POOL_PAD = 16
N_HEADS = 8
HEAD_DIM = 64
LANES = 128
TOPK = 16
CAND_ROWS = tuple((a, TOPK // (a + 1)) for a in range(TOPK))
VMEM_LIMIT = 56 * 1024 * 1024
INV_SQRT2 = 1.0 / math.sqrt(2.0)
LOG2E = 1.0 / math.log(2.0)
NEG_INF = float("-inf")


def _cparams(sem):
    return pltpu.CompilerParams(dimension_semantics=sem, vmem_limit_bytes=VMEM_LIMIT)


def _rms(x):
    return x * lax.rsqrt(jnp.mean(x * x, axis=-1, keepdims=True) + NORM_EPS)


def _ada_kernel(c_ref, w_ref, b_ref, o_ref):
    c = c_ref[...]
    a = (c * jax.nn.sigmoid(c)).astype(BF16)
    o_ref[...] = jnp.dot(a, w_ref[...], preferred_element_type=F32) + b_ref[...]


def _ada(c, w_bf, b):
    n, d = c.shape
    nout = w_bf.shape[1]
    return pl.pallas_call(
        _ada_kernel,
        grid=(nout // d,),
        in_specs=[pl.BlockSpec((n, d), lambda j: (0, 0)),
                  pl.BlockSpec((d, d), lambda j: (0, j)),
                  pl.BlockSpec((1, d), lambda j: (0, j))],
        out_specs=pl.BlockSpec((n, d), lambda j: (0, j)),
        out_shape=jax.ShapeDtypeStruct((n, nout), F32),
        compiler_params=_cparams(("parallel",)),
        name="ada",
    )(c, w_bf, b)


def _front_kernel(x_ref, sc_ref, sh_ref, g_ref, w_ref, cos_ref, sa_ref, sb_ref,
                  p_ref, q_ref, k_ref, v_ref, *, d, pool_dim):
    x = x_ref[0]
    h = (_rms(x) * g_ref[...]) * (1.0 + sc_ref[0]) + sh_ref[0]
    proj = jnp.dot(h.astype(BF16), w_ref[...], preferred_element_type=F32)
    p_ref[0] = proj[:, :pool_dim]
    cos, sa, sb = cos_ref[...], sa_ref[...], sb_ref[...]
    half = HEAD_DIM // 2

    def rope(t):
        return (t * cos + pltpu.roll(t, LANES - half, axis=1) * sa
                + pltpu.roll(t, half, axis=1) * sb)

    qscale = HEAD_DIM ** -0.5 * LOG2E
    for j in range(d // LANES):
        o = pool_dim + j * LANES
        q_ref[0, :, j * LANES:(j + 1) * LANES] = (rope(proj[:, o:o + LANES]) * qscale).astype(BF16)
        o = pool_dim + d + j * LANES
        k_ref[0, :, j * LANES:(j + 1) * LANES] = rope(proj[:, o:o + LANES]).astype(BF16)
    v_ref[0] = proj[:, pool_dim + 2 * d:].astype(BF16)


def _front(x, sc, sh, g, w_in_bf, cos, sa, sb, tm):
    b, s, d = x.shape
    in_dim = w_in_bf.shape[1]
    pool_dim = in_dim - 3 * d
    row = lambda i, j: (i, j, 0)
    per_b = lambda i, j: (i, 0, 0)
    const = lambda i, j: (0, 0)
    return pl.pallas_call(
        functools.partial(_front_kernel, d=d, pool_dim=pool_dim),
        grid=(b, s // tm),
        in_specs=[pl.BlockSpec((1, tm, d), row),
                  pl.BlockSpec((1, 1, d), per_b),
                  pl.BlockSpec((1, 1, d), per_b),
                  pl.BlockSpec((1, d), const),
                  pl.BlockSpec((d, in_dim), const),
                  pl.BlockSpec((tm, LANES), lambda i, j: (j, 0)),
                  pl.BlockSpec((tm, LANES), lambda i, j: (j, 0)),
                  pl.BlockSpec((tm, LANES), lambda i, j: (j, 0))],
        out_specs=[pl.BlockSpec((1, tm, pool_dim), row),
                   pl.BlockSpec((1, tm, d), row),
                   pl.BlockSpec((1, tm, d), row),
                   pl.BlockSpec((1, tm, d), row)],
        out_shape=[jax.ShapeDtypeStruct((b, s, pool_dim), F32),
                   jax.ShapeDtypeStruct((b, s, d), BF16),
                   jax.ShapeDtypeStruct((b, s, d), BF16),
                   jax.ShapeDtypeStruct((b, s, d), BF16)],
        compiler_params=_cparams(("parallel", "parallel")),
        name="front",
    )(x, sc, sh, g, w_in_bf, cos, sa, sb)


def _pool_kernel(p_ref, w_ref, scale_ref, o_ref, pad_ref, *, s):
    pos = lax.broadcasted_iota(I32, (s, LANES), 0)
    zeros = jnp.zeros((POOL_PAD, LANES), F32)
    for g, win in enumerate(POOL_WINDOWS):
        half = win // 2
        xg = p_ref[0, :, g * LANES:(g + 1) * LANES]
        pad_ref[0:POOL_PAD, :] = zeros
        pad_ref[POOL_PAD:POOL_PAD + s, :] = xg
        pad_ref[POOL_PAD + s:2 * POOL_PAD + s, :] = zeros
        wsum = pad_ref[POOL_PAD - half:POOL_PAD - half + s, :]
        for dlt in range(-half + 1, half):
            wsum = wsum + pad_ref[POOL_PAD + dlt:POOL_PAD + dlt + s, :]
        cnt = (jnp.minimum(pos + half, s) - jnp.maximum(pos - half, 0)).astype(F32)
        diff = wsum / cnt - xg
        mixed = jnp.dot(diff.astype(BF16), w_ref[g], preferred_element_type=F32)
        o_ref[0, :, g * LANES:(g + 1) * LANES] = (
            mixed * scale_ref[:, g * LANES:(g + 1) * LANES]).astype(BF16)


def _pool(p, pool_w_bf, pool_scale):
    b, s, pool_dim = p.shape
    ng = pool_w_bf.shape[0]
    return pl.pallas_call(
        functools.partial(_pool_kernel, s=s),
        grid=(b,),
        in_specs=[pl.BlockSpec((1, s, pool_dim), lambda i: (i, 0, 0)),
                  pl.BlockSpec((ng, LANES, LANES), lambda i: (0, 0, 0)),
                  pl.BlockSpec((1, pool_dim), lambda i: (0, 0))],
        out_specs=pl.BlockSpec((1, s, pool_dim), lambda i: (i, 0, 0)),
        out_shape=jax.ShapeDtypeStruct((b, s, pool_dim), BF16),
        scratch_shapes=[pltpu.VMEM((s + 2 * POOL_PAD, LANES), F32)],
        compiler_params=_cparams(("parallel",)),
        name="pool",
    )(p, pool_w_bf, pool_scale)


def _attn_kernel(lamv_ref, q_ref, k_ref, v_ref, gsub_ref, o_ref, s_scr, vext_scr,
                 *, lam_init, s, tq, tk):
    lv = lamv_ref[...]
    lam = (jnp.exp(jnp.sum(lv[0:1] * lv[1:2], axis=-1, keepdims=True))
           - jnp.exp(jnp.sum(lv[2:3] * lv[3:4], axis=-1, keepdims=True)) + lam_init)
    @pl.when(pl.program_id(2) == 0)
    def _extend_values():
        vext_scr[:, :LANES] = v_ref[0]
        vext_scr[:, LANES:] = jnp.ones((s, LANES), BF16)

    q = q_ref[0]
    first = lax.broadcasted_iota(I32, (1, LANES), 1) < HEAD_DIM
    zero = jnp.zeros_like(q)
    qs = (jnp.where(first, q, zero), jnp.where(first, zero, q))
    nt = (((1,), (1,)), ((), ()))
    nchunk = s // tk
    mx = [jnp.full((tq, LANES), NEG_INF, F32), jnp.full((tq, LANES), NEG_INF, F32)]
    for c in range(nchunk):
        kc = k_ref[0, c * tk:(c + 1) * tk, :]
        for u in range(2):
            sc = lax.dot_general(qs[u], kc, nt, preferred_element_type=F32)
            s_scr[u, :, c * tk:(c + 1) * tk] = sc
            for jj in range(tk // LANES):
                mx[u] = jnp.maximum(mx[u], sc[:, jj * LANES:(jj + 1) * LANES])
    mx = [jnp.max(m, axis=-1, keepdims=True) for m in mx]
    acc = [jnp.zeros((tq, 2 * LANES), F32), jnp.zeros((tq, 2 * LANES), F32)]
    for c in range(nchunk):
        vc = vext_scr[c * tk:(c + 1) * tk, :]
        for u in range(2):
            p = jnp.exp2(s_scr[u, :, c * tk:(c + 1) * tk] - mx[u]).astype(BF16)
            acc[u] = acc[u] + jnp.dot(p, vc, preferred_element_type=F32)
    c1 = 1.0 / acc[0][:, LANES:LANES + 1]
    c2 = lam / acc[1][:, LANES:LANES + 1]
    o = acc[0][:, :LANES] * c1 - acc[1][:, :LANES] * c2
    o_ref[0] = ((_rms(o) * gsub_ref[...]) * (1.0 - lam_init)).astype(BF16)


def _attention(q, k, v, lamv, g_sub, lam_init, tq, tk):
    b, s, d = q.shape
    nh = d // LANES
    return pl.pallas_call(
        functools.partial(_attn_kernel, lam_init=lam_init, s=s, tq=tq, tk=tk),
        grid=(b, nh, s // tq),
        in_specs=[pl.BlockSpec((4, HEAD_DIM), lambda i, h, j: (0, 0)),
                  pl.BlockSpec((1, tq, LANES), lambda i, h, j: (i, j, h)),
                  pl.BlockSpec((1, s, LANES), lambda i, h, j: (i, 0, h)),
                  pl.BlockSpec((1, s, LANES), lambda i, h, j: (i, 0, h)),
                  pl.BlockSpec((1, LANES), lambda i, h, j: (0, 0))],
        out_specs=pl.BlockSpec((1, tq, LANES), lambda i, h, j: (i, j, h)),
        out_shape=jax.ShapeDtypeStruct((b, s, d), BF16),
        scratch_shapes=[pltpu.VMEM((2, tq, s), F32), pltpu.VMEM((s, 2 * LANES), BF16)],
        compiler_params=_cparams(("parallel", "parallel", "arbitrary")),
        name="attn",
    )(lamv, q, k, v, g_sub)


def _merge_kernel(x_ref, mixed_ref, att_ref, sc1_ref, sh1_ref, gt1_ref, sc2_ref, sh2_ref,
                  g1_ref, g2_ref, wpool_ref, watt_ref, wgate_ref, bgate_ref, wout_ref,
                  x1_ref, h2_ref, *, d):
    x = x_ref[0]
    h = ((_rms(x) * g1_ref[...]) * (1.0 + sc1_ref[0]) + sh1_ref[0]).astype(BF16)
    gates = jax.nn.sigmoid(jnp.dot(h, wgate_ref[...], preferred_element_type=F32) + bgate_ref[...])
    y_pool = jnp.dot(mixed_ref[0], wpool_ref[...], preferred_element_type=F32)
    y_att = jnp.dot(att_ref[0], watt_ref[...], preferred_element_type=F32)
    merged = gates[:, :d] * y_pool + gates[:, d:] * y_att
    x1 = x + gt1_ref[0] * jnp.dot(merged.astype(BF16), wout_ref[...], preferred_element_type=F32)
    x1_ref[0] = x1
    h2 = (_rms(x1) * g2_ref[...]) * (1.0 + sc2_ref[0]) + sh2_ref[0]
    h2_ref[0] = h2.astype(BF16)


def _merge(x, mixed, att, mods, g1, g2, wpool, watt, wgate, bgate, wout, tm):
    b, s, d = x.shape
    pool_dim = mixed.shape[-1]
    row = lambda i, j: (i, j, 0)
    per_b = lambda i, j: (i, 0, 0)
    const = lambda i, j: (0, 0)
    sc1, sh1, gt1, sc2, sh2 = mods
    return pl.pallas_call(
        functools.partial(_merge_kernel, d=d),
        grid=(b, s // tm),
        in_specs=[pl.BlockSpec((1, tm, d), row),
                  pl.BlockSpec((1, tm, pool_dim), row),
                  pl.BlockSpec((1, tm, d), row)]
                 + [pl.BlockSpec((1, 1, d), per_b)] * 5
                 + [pl.BlockSpec((1, d), const), pl.BlockSpec((1, d), const),
                    pl.BlockSpec((pool_dim, d), const), pl.BlockSpec((d, d), const),
                    pl.BlockSpec((d, 2 * d), const), pl.BlockSpec((1, 2 * d), const),
                    pl.BlockSpec((d, d), const)],
        out_specs=[pl.BlockSpec((1, tm, d), row), pl.BlockSpec((1, tm, d), row)],
        out_shape=[jax.ShapeDtypeStruct((b, s, d), F32), jax.ShapeDtypeStruct((b, s, d), BF16)],
        compiler_params=_cparams(("parallel", "parallel")),
        name="merge",
    )(x, mixed, att, sc1, sh1, gt1, sc2, sh2, g1, g2, wpool, watt, wgate, bgate, wout)


ROUTE_TOKENS = 8 * LANES
LAY_PITCH = LANES + 8


def _sort16_network():
    n, pairs, p = TOPK, [], 1
    while p < n:
        k = p
        while k >= 1:
            for j in range(k % p, n - k, 2 * k):
                for i in range(min(k, n - j - k)):
                    if (i + j) // (2 * p) == (i + j + k) // (2 * p):
                        pairs.append((i + j, i + j + k))
            k //= 2
        p *= 2
    return tuple(pairs)


SORT16 = _sort16_network()


def _swap_mask(a, b):
    (av, ai), (bv, bi) = a, b
    return (bv > av) | ((bv == av) & (bi < ai))


def _order(a, b):
    sw = _swap_mask(a, b)
    first = (jnp.maximum(a[0], b[0]), jnp.where(sw, b[1], a[1]))
    second = (jnp.minimum(a[0], b[0]), jnp.where(sw, a[1], b[1]))
    return first, second


def _first_of(a, b):
    return jnp.maximum(a[0], b[0]), jnp.where(_swap_mask(a, b), b[1], a[1])


def _sort16(items):
    items = list(items)
    for i, j in SORT16:
        items[i], items[j] = _order(items[i], items[j])
    return items


def _merge_top16(a, b):
    c = [_first_of(a[i], b[TOPK - 1 - i]) for i in range(TOPK)]
    stride = TOPK // 2
    while stride:
        for i in range(TOPK):
            if not i & stride:
                c[i], c[i + stride] = _order(c[i], c[i + stride])
        stride //= 2
    return c


def _route_kernel(h2_ref, wqt_ref, keys_ref, e1_ref, e2_ref, gate_ref,
                  qt_scr, lay_scr, tv_scr, ti_scr):
    nblk = ROUTE_TOKENS // LANES
    qt = lax.dot_general(wqt_ref[...], h2_ref[...], (((1,), (1,)), ((), ())),
                         preferred_element_type=F32)
    qt_scr[...] = qt.astype(BF16)

    def set_body(si, carry):
        r0 = pl.multiple_of(si * HEAD_DIM, HEAD_DIM)
        st = jnp.dot(keys_ref[si], qt_scr[pl.ds(r0, HEAD_DIM), :], preferred_element_type=F32)
        for c in range(nblk):
            lay_scr[c * LAY_PITCH:c * LAY_PITCH + LANES, :] = st[:, c * LANES:(c + 1) * LANES]
        lists = []
        for g in range(LANES // TOPK):
            items = []
            for k in range(g * TOPK, (g + 1) * TOPK):
                items.append((lay_scr[pl.ds(k, nblk, stride=LAY_PITCH), :],
                              jnp.full((nblk, LANES), k, I32)))
            lists.append(_sort16(items))
        while len(lists) > 1:
            lists = [_merge_top16(lists[2 * i], lists[2 * i + 1]) for i in range(len(lists) // 2)]
        for a, (v, idx) in enumerate(lists[0]):
            tv_scr[si * TOPK + a] = v
            ti_scr[si * TOPK + a] = idx
        return carry

    lax.fori_loop(0, 2 * N_HEADS, set_body, 0)

    def head_body(h, carry):
        b1 = 2 * h * TOPK
        b2 = b1 + TOPK
        cv, ce = [], []
        for a, nb in CAND_ROWS:
            va = tv_scr[b1 + a]
            ea = ti_scr[b1 + a] * LANES
            for b in range(nb):
                cv.append(va + tv_scr[b2 + b])
                ce.append(ea + ti_scr[b2 + b])
        best_s, best_e = [], []
        for _ in range(TOPK):
            nodes = list(zip(cv, ce))
            while len(nodes) > 1:
                nxt = []
                for i in range(0, len(nodes) - 1, 2):
                    (lv, le), (rv, re) = nodes[i], nodes[i + 1]
                    nxt.append((jnp.maximum(lv, rv), jnp.where(rv > lv, re, le)))
                if len(nodes) % 2:
                    nxt.append(nodes[-1])
                nodes = nxt
            m, e = nodes[0]
            best_s.append(m)
            best_e.append(e)
            cv = [jnp.where(c_e == e, NEG_INF, c_v) for c_v, c_e in zip(cv, ce)]
        ex = [jnp.exp(b - best_s[0]) for b in best_s]
        z = ex[0]
        for t in ex[1:]:
            z = z + t
        for k in range(TOPK):
            e1_ref[h * TOPK + k] = lax.shift_right_logical(best_e[k], 7)
            e2_ref[h * TOPK + k] = best_e[k] & (LANES - 1)
            gate_ref[h * TOPK + k] = ex[k] / z
        return carry

    lax.fori_loop(0, N_HEADS, head_body, 0)


def _route(h2_flat, wqt_bf, keys_bf):
    t, d = h2_flat.shape
    assert t % ROUTE_TOKENS == 0
    nslot = N_HEADS * TOPK
    nblk = ROUTE_TOKENS // LANES
    keys2 = keys_bf.reshape(2 * N_HEADS, LANES, HEAD_DIM)
    out = pl.BlockSpec((nslot, nblk, LANES), lambda i: (0, i, 0))
    outs = pl.pallas_call(
        _route_kernel,
        grid=(t // ROUTE_TOKENS,),
        in_specs=[pl.BlockSpec((ROUTE_TOKENS, d), lambda i: (i, 0)),
                  pl.BlockSpec(wqt_bf.shape, lambda i: (0, 0)),
                  pl.BlockSpec(keys2.shape, lambda i: (0, 0, 0))],
        out_specs=[out, out, out],
        out_shape=[jax.ShapeDtypeStruct((nslot, t // LANES, LANES), I32),
                   jax.ShapeDtypeStruct((nslot, t // LANES, LANES), I32),
                   jax.ShapeDtypeStruct((nslot, t // LANES, LANES), F32)],
        scratch_shapes=[pltpu.VMEM((wqt_bf.shape[0], ROUTE_TOKENS), BF16),
                        pltpu.VMEM((nblk * LAY_PITCH, LANES), F32),
                        pltpu.VMEM((2 * N_HEADS * TOPK, nblk, LANES), F32),
                        pltpu.VMEM((2 * N_HEADS * TOPK, nblk, LANES), I32)],
        compiler_params=_cparams(("parallel",)),
        name="route",
    )(h2_flat, wqt_bf, keys2)
    return [o.reshape(nslot, t).T for o in outs]


GATE_ROWS = LANES // 2
GATE_PITCH = GATE_ROWS + 4
BUILD_GROUP = 8
HI_MASK = -65536


def _peer_kernel(h2_ref, e1_ref, e2_ref, gate_ref, ut_ref, v_ref, x1_ref, gt2_ref, gf_ref,
                 y_ref, gmat_scr, w_scr, acc_scr, *, tp, ec):
    c = pl.program_id(1)
    nsub = ec // LANES

    @pl.when(c == 0)
    def _build_gate_matrices():
        acc_scr[...] = jnp.zeros_like(acc_scr)
        key_id = lax.broadcasted_iota(I32, (LANES, LANES), 0)

        def body(tb, carry):
            base = pl.multiple_of(tb * BUILD_GROUP, BUILD_GROUP)
            r1 = e1_ref[pl.ds(base, BUILD_GROUP), :]
            r2 = e2_ref[pl.ds(base, BUILD_GROUP), :]
            gg = gate_ref[pl.ds(base, BUILD_GROUP), :]
            for i in range(BUILD_GROUP):
                a = jnp.where(key_id == r1[i:i + 1, :], gg[i:i + 1, :], 0.0).astype(BF16)
                b = jnp.where(key_id == r2[i:i + 1, :], 1.0, 0.0).astype(BF16)
                gm = lax.dot_general(a, b, (((1,), (1,)), ((), ())), preferred_element_type=F32)
                bits = lax.bitcast_convert_type(gm, I32) + 32768
                packed = (bits[:GATE_ROWS] & HI_MASK) | lax.shift_right_logical(bits[GATE_ROWS:], 16)
                gmat_scr[pl.ds((base + i) * GATE_PITCH, GATE_ROWS), :] = packed
            return carry

        lax.fori_loop(0, tp // BUILD_GROUP, body, 0)

    s = jnp.dot(h2_ref[...], ut_ref[...], preferred_element_type=F32)
    for j in range(nsub):
        k1 = c * nsub + j
        word = gmat_scr[pl.ds(k1 % GATE_ROWS, tp, stride=GATE_PITCH), :]
        shift = (k1 // GATE_ROWS) * 16
        gi = lax.bitcast_convert_type(lax.shift_left(word, shift) & HI_MASK, F32)
        sj = s[:, j * LANES:(j + 1) * LANES]
        w = (0.5 * sj) * (1.0 + lax.erf(sj * INV_SQRT2)) * gi
        w_scr[:, j * LANES:(j + 1) * LANES] = w.astype(BF16)
    acc_scr[...] += jnp.dot(w_scr[...], v_ref[...], preferred_element_type=F32)

    @pl.when(c == pl.num_programs(1) - 1)
    def _finish():
        xo = x1_ref[...] + gt2_ref[0] * acc_scr[...]
        y_ref[...] = _rms(xo) * gf_ref[...]


def _peer(h2_flat, e1, e2, gate, ut_bf, v_bf, x1_flat, gt2, g_final, tokens_per_batch, tp, ec):
    t, d = h2_flat.shape
    ne = v_bf.shape[0]
    nslot = e1.shape[1]
    tiles_per_batch = tokens_per_batch // tp
    tok = lambda i, c: (i, 0)
    return pl.pallas_call(
        functools.partial(_peer_kernel, tp=tp, ec=ec),
        grid=(t // tp, ne // ec),
        in_specs=[pl.BlockSpec((tp, d), tok),
                  pl.BlockSpec((tp, nslot), tok),
                  pl.BlockSpec((tp, nslot), tok),
                  pl.BlockSpec((tp, nslot), tok),
                  pl.BlockSpec((d, ec), lambda i, c: (0, c)),
                  pl.BlockSpec((ec, d), lambda i, c: (c, 0)),
                  pl.BlockSpec((tp, d), tok),
                  pl.BlockSpec((1, 1, d), lambda i, c: (i // tiles_per_batch, 0, 0)),
                  pl.BlockSpec((1, d), lambda i, c: (0, 0))],
        out_specs=pl.BlockSpec((tp, d), tok),
        out_shape=jax.ShapeDtypeStruct((t, d), F32),
        scratch_shapes=[pltpu.VMEM((tp * GATE_PITCH, LANES), I32),
                        pltpu.VMEM((tp, ec), BF16),
                        pltpu.VMEM((tp, d), F32)],
        compiler_params=_cparams(("parallel", "arbitrary")),
        name="peer",
    )(h2_flat, e1, e2, gate, ut_bf, v_bf, x1_flat, gt2, g_final)


def _rope_tables(s):
    inv_freq = 1.0 / (ROPE_THETA ** (jnp.arange(0, HEAD_DIM, 2, dtype=F32) / HEAD_DIM))
    ang = jnp.arange(s, dtype=F32)[:, None] * inv_freq[None, :]
    ang = jnp.concatenate([ang, ang, ang, ang], axis=-1)
    cos, sin = jnp.cos(ang), jnp.sin(ang)
    low = (jnp.arange(LANES) % HEAD_DIM) < (HEAD_DIM // 2)
    return cos, jnp.where(low, -sin, 0.0), jnp.where(low, 0.0, sin)


def _pick_tile(n, want):
    t = min(n, want)
    while n % t:
        t //= 2
    return t


def _trunk(x, mod, wts, lam_init):
    b, s, d = x.shape
    sh1, sc1, gt1, sh2, sc2, gt2 = [m.reshape(b, 1, d) for m in jnp.split(mod, 6, axis=-1)]
    cos, sa, sb = _rope_tables(s)
    p, q, k, v = _front(x, sc1, sh1, wts["g_norm1"], wts["w_in"], cos, sa, sb, _pick_tile(s, 512))
    mixed = _pool(p, wts["pool_w"], wts["pool_scale"])
    att = _attention(q, k, v, wts["lamv"], wts["g_sub"], lam_init, _pick_tile(s, 512),
                     _pick_tile(s, 512))
    x1, h2 = _merge(x, mixed, att, (sc1, sh1, gt1, sc2, sh2), wts["g_norm1"], wts["g_norm2"],
                    wts["w_pool_out"], wts["w_att_out"], wts["w_gate"], wts["b_gate"],
                    wts["w_out"], _pick_tile(s, 512))
    t = b * s
    h2f = h2.reshape(t, d)
    e1, e2, gate = _route(h2f, wts["w_peer_qt"], wts["peer_keys"])
    y = _peer(h2f, e1, e2, gate, wts["peer_ut"], wts["peer_v"], x1.reshape(t, d),
              gt2, wts["g_final"], s, _pick_tile(s, 512), 1024)
    return y.reshape(b, s, d)


def kernel(x_prompt, x_sample, c_prompt, c_sample, w_ada, b_ada, g_norm1, w_in, pool_w, pool_scale,
           w_pool_out, lam_q1, lam_k1, lam_q2, lam_k2, g_sub, w_att_out, w_gate, b_gate, w_out,
           g_norm2, w_peer_q, peer_keys, peer_u, peer_v, g_final):
    depth = w_ada.shape[0]
    assert depth == 1, "single-layer trunk"
    l = 0
    lam_init = 0.8 - 0.6 * math.exp(-0.3 * l)
    d = x_prompt.shape[-1]
    wts = {
        "g_norm1": g_norm1[l][None, :], "g_norm2": g_norm2[l][None, :], "g_final": g_final[None, :],
        "w_in": w_in[l].astype(BF16),
        "pool_w": pool_w[l].astype(BF16), "pool_scale": pool_scale[l][None, :],
        "w_pool_out": w_pool_out[l].astype(BF16),
        "lamv": jnp.stack([lam_q1[l], lam_k1[l], lam_q2[l], lam_k2[l]]).astype(F32),
        "g_sub": g_sub[l][None, :],
        "w_att_out": w_att_out[l].astype(BF16),
        "w_gate": w_gate[l].astype(BF16), "b_gate": b_gate[l][None, :],
        "w_out": w_out[l].astype(BF16),
        "w_peer_qt": w_peer_q[l].T.astype(BF16),
        "peer_keys": peer_keys[l].astype(BF16),
        "peer_ut": peer_u[l].T.astype(BF16),
        "peer_v": peer_v[l].astype(BF16),
    }
    nb = x_prompt.shape[0]
    mod = _ada(jnp.concatenate([c_prompt, c_sample], axis=0), w_ada[l].astype(BF16), b_ada[l][None, :])
    y_prompt = _trunk(x_prompt, mod[:nb], wts, lam_init)
    y_sample = _trunk(x_sample, mod[nb:], wts, lam_init)
    return (y_prompt, y_sample)
```

```python
import functools
import math

import jax
import jax.numpy as jnp
from jax import lax
from jax.experimental import pallas as pl
from jax.experimental.pallas import tpu as pltpu

F32 = jnp.float32
BF16 = jnp.bfloat16
I32 = jnp.int32

NORM_EPS = 1e-6
ROPE_THETA = 10000.0
POOL_WINDOWS = (2, 4, 8, 16)
POOL_PAD = 16
N_HEADS = 8
HEAD_DIM = 64
LANES = 128
TOPK = 16
CAND_ROWS = tuple((a, TOPK // (a + 1)) for a in range(TOPK))
VMEM_LIMIT = 56 * 1024 * 1024
INV_SQRT2 = 1.0 / math.sqrt(2.0)
LOG2E = 1.0 / math.log(2.0)
NEG_INF = float("-inf")


def _cparams(sem):
    return pltpu.CompilerParams(dimension_semantics=sem, vmem_limit_bytes=VMEM_LIMIT)


def _rms(x):
    return x * lax.rsqrt(jnp.mean(x * x, axis=-1, keepdims=True) + NORM_EPS)


def _ada_kernel(c_ref, w_ref, b_ref, o_ref):
    c = c_ref[...]
    a = (c * jax.nn.sigmoid(c)).astype(BF16)
    o_ref[...] = jnp.dot(a, w_ref[...], preferred_element_type=F32) + b_ref[...]


def _ada(c, w_bf, b):
    n, d = c.shape
    nout = w_bf.shape[1]
    return pl.pallas_call(
        _ada_kernel,
        grid=(nout // d,),
        in_specs=[pl.BlockSpec((n, d), lambda j: (0, 0)),
                  pl.BlockSpec((d, d), lambda j: (0, j)),
                  pl.BlockSpec((1, d), lambda j: (0, j))],
        out_specs=pl.BlockSpec((n, d), lambda j: (0, j)),
        out_shape=jax.ShapeDtypeStruct((n, nout), F32),
        compiler_params=_cparams(("parallel",)),
        name="ada",
    )(c, w_bf, b)


def _front_kernel(x_ref, sc_ref, sh_ref, g_ref, w_ref, cos_ref, sa_ref, sb_ref,
                  p_ref, q_ref, k_ref, v_ref, *, d, pool_dim):
    x = x_ref[0]
    h = (_rms(x) * g_ref[...]) * (1.0 + sc_ref[0]) + sh_ref[0]
    proj = jnp.dot(h.astype(BF16), w_ref[...], preferred_element_type=F32)
    p_ref[0] = proj[:, :pool_dim]
    cos, sa, sb = cos_ref[...], sa_ref[...], sb_ref[...]
    half = HEAD_DIM // 2

    def rope(t):
        return (t * cos + pltpu.roll(t, LANES - half, axis=1) * sa
                + pltpu.roll(t, half, axis=1) * sb)

    qscale = HEAD_DIM ** -0.5 * LOG2E
    for j in range(d // LANES):
        o = pool_dim + j * LANES
        q_ref[0, :, j * LANES:(j + 1) * LANES] = (rope(proj[:, o:o + LANES]) * qscale).astype(BF16)
        o = pool_dim + d + j * LANES
        k_ref[0, :, j * LANES:(j + 1) * LANES] = rope(proj[:, o:o + LANES]).astype(BF16)
    v_ref[0] = proj[:, pool_dim + 2 * d:].astype(BF16)


def _front(x, sc, sh, g, w_in_bf, cos, sa, sb, tm):
    b, s, d = x.shape
    in_dim = w_in_bf.shape[1]
    pool_dim = in_dim - 3 * d
    row = lambda i, j: (i, j, 0)
    per_b = lambda i, j: (i, 0, 0)
    const = lambda i, j: (0, 0)
    return pl.pallas_call(
        functools.partial(_front_kernel, d=d, pool_dim=pool_dim),
        grid=(b, s // tm),
        in_specs=[pl.BlockSpec((1, tm, d), row),
                  pl.BlockSpec((1, 1, d), per_b),
                  pl.BlockSpec((1, 1, d), per_b),
                  pl.BlockSpec((1, d), const),
                  pl.BlockSpec((d, in_dim), const),
                  pl.BlockSpec((tm, LANES), lambda i, j: (j, 0)),
                  pl.BlockSpec((tm, LANES), lambda i, j: (j, 0)),
                  pl.BlockSpec((tm, LANES), lambda i, j: (j, 0))],
        out_specs=[pl.BlockSpec((1, tm, pool_dim), row),
                   pl.BlockSpec((1, tm, d), row),
                   pl.BlockSpec((1, tm, d), row),
                   pl.BlockSpec((1, tm, d), row)],
        out_shape=[jax.ShapeDtypeStruct((b, s, pool_dim), F32),
                   jax.ShapeDtypeStruct((b, s, d), BF16),
                   jax.ShapeDtypeStruct((b, s, d), BF16),
                   jax.ShapeDtypeStruct((b, s, d), BF16)],
        compiler_params=_cparams(("parallel", "parallel")),
        name="front",
    )(x, sc, sh, g, w_in_bf, cos, sa, sb)


def _pool_kernel(p_ref, w_ref, scale_ref, o_ref, pad_ref, *, s):
    pos = lax.broadcasted_iota(I32, (s, LANES), 0)
    zeros = jnp.zeros((POOL_PAD, LANES), F32)
    for g, win in enumerate(POOL_WINDOWS):
        half = win // 2
        xg = p_ref[0, :, g * LANES:(g + 1) * LANES]
        pad_ref[0:POOL_PAD, :] = zeros
        pad_ref[POOL_PAD:POOL_PAD + s, :] = xg
        pad_ref[POOL_PAD + s:2 * POOL_PAD + s, :] = zeros
        wsum = pad_ref[POOL_PAD - half:POOL_PAD - half + s, :]
        for dlt in range(-half + 1, half):
            wsum = wsum + pad_ref[POOL_PAD + dlt:POOL_PAD + dlt + s, :]
        cnt = (jnp.minimum(pos + half, s) - jnp.maximum(pos - half, 0)).astype(F32)
        diff = wsum / cnt - xg
        mixed = jnp.dot(diff.astype(BF16), w_ref[g], preferred_element_type=F32)
        o_ref[0, :, g * LANES:(g + 1) * LANES] = (
            mixed * scale_ref[:, g * LANES:(g + 1) * LANES]).astype(BF16)


def _pool(p, pool_w_bf, pool_scale):
    b, s, pool_dim = p.shape
    ng = pool_w_bf.shape[0]
    return pl.pallas_call(
        functools.partial(_pool_kernel, s=s),
        grid=(b,),
        in_specs=[pl.BlockSpec((1, s, pool_dim), lambda i: (i, 0, 0)),
                  pl.BlockSpec((ng, LANES, LANES), lambda i: (0, 0, 0)),
                  pl.BlockSpec((1, pool_dim), lambda i: (0, 0))],
        out_specs=pl.BlockSpec((1, s, pool_dim), lambda i: (i, 0, 0)),
        out_shape=jax.ShapeDtypeStruct((b, s, pool_dim), BF16),
        scratch_shapes=[pltpu.VMEM((s + 2 * POOL_PAD, LANES), F32)],
        compiler_params=_cparams(("parallel",)),
        name="pool",
    )(p, pool_w_bf, pool_scale)


def _attn_kernel(lamv_ref, q_ref, k_ref, v_ref, gsub_ref, o_ref, s_scr, vext_scr,
                 *, lam_init, s, tq, tk):
    lv = lamv_ref[...]
    lam = (jnp.exp(jnp.sum(lv[0:1] * lv[1:2], axis=-1, keepdims=True))
           - jnp.exp(jnp.sum(lv[2:3] * lv[3:4], axis=-1, keepdims=True)) + lam_init)
    @pl.when(pl.program_id(2) == 0)
    def _extend_values():
        vext_scr[:, :LANES] = v_ref[0]
        vext_scr[:, LANES:] = jnp.ones((s, LANES), BF16)

    q = q_ref[0]
    first = lax.broadcasted_iota(I32, (1, LANES), 1) < HEAD_DIM
    zero = jnp.zeros_like(q)
    qs = (jnp.where(first, q, zero), jnp.where(first, zero, q))
    nt = (((1,), (1,)), ((), ()))
    nchunk = s // tk
    mx = [jnp.full((tq, LANES), NEG_INF, F32), jnp.full((tq, LANES), NEG_INF, F32)]
    for c in range(nchunk):
        kc = k_ref[0, c * tk:(c + 1) * tk, :]
        for u in range(2):
            sc = lax.dot_general(qs[u], kc, nt, preferred_element_type=F32)
            s_scr[u, :, c * tk:(c + 1) * tk] = sc
            for jj in range(tk // LANES):
                mx[u] = jnp.maximum(mx[u], sc[:, jj * LANES:(jj + 1) * LANES])
    mx = [jnp.max(m, axis=-1, keepdims=True) for m in mx]
    acc = [jnp.zeros((tq, 2 * LANES), F32), jnp.zeros((tq, 2 * LANES), F32)]
    for c in range(nchunk):
        vc = vext_scr[c * tk:(c + 1) * tk, :]
        for u in range(2):
            p = jnp.exp2(s_scr[u, :, c * tk:(c + 1) * tk] - mx[u]).astype(BF16)
            acc[u] = acc[u] + jnp.dot(p, vc, preferred_element_type=F32)
    c1 = 1.0 / acc[0][:, LANES:LANES + 1]
    c2 = lam / acc[1][:, LANES:LANES + 1]
    o = acc[0][:, :LANES] * c1 - acc[1][:, :LANES] * c2
    o_ref[0] = ((_rms(o) * gsub_ref[...]) * (1.0 - lam_init)).astype(BF16)


def _attention(q, k, v, lamv, g_sub, lam_init, tq, tk):
    b, s, d = q.shape
    nh = d // LANES
    return pl.pallas_call(
        functools.partial(_attn_kernel, lam_init=lam_init, s=s, tq=tq, tk=tk),
        grid=(b, nh, s // tq),
        in_specs=[pl.BlockSpec((4, HEAD_DIM), lambda i, h, j: (0, 0)),
                  pl.BlockSpec((1, tq, LANES), lambda i, h, j: (i, j, h)),
                  pl.BlockSpec((1, s, LANES), lambda i, h, j: (i, 0, h)),
                  pl.BlockSpec((1, s, LANES), lambda i, h, j: (i, 0, h)),
                  pl.BlockSpec((1, LANES), lambda i, h, j: (0, 0))],
        out_specs=pl.BlockSpec((1, tq, LANES), lambda i, h, j: (i, j, h)),
        out_shape=jax.ShapeDtypeStruct((b, s, d), BF16),
        scratch_shapes=[pltpu.VMEM((2, tq, s), F32), pltpu.VMEM((s, 2 * LANES), BF16)],
        compiler_params=_cparams(("parallel", "parallel", "arbitrary")),
        name="attn",
    )(lamv, q, k, v, g_sub)


def _merge_kernel(x_ref, mixed_ref, att_ref, sc1_ref, sh1_ref, gt1_ref, sc2_ref, sh2_ref,
                  g1_ref, g2_ref, wpool_ref, watt_ref, wgate_ref, bgate_ref, wout_ref,
                  x1_ref, h2_ref, *, d):
    x = x_ref[0]
    h = ((_rms(x) * g1_ref[...]) * (1.0 + sc1_ref[0]) + sh1_ref[0]).astype(BF16)
    gates = jax.nn.sigmoid(jnp.dot(h, wgate_ref[...], preferred_element_type=F32) + bgate_ref[...])
    y_pool = jnp.dot(mixed_ref[0], wpool_ref[...], preferred_element_type=F32)
    y_att = jnp.dot(att_ref[0], watt_ref[...], preferred_element_type=F32)
    merged = gates[:, :d] * y_pool + gates[:, d:] * y_att
    x1 = x + gt1_ref[0] * jnp.dot(merged.astype(BF16), wout_ref[...], preferred_element_type=F32)
    x1_ref[0] = x1
    h2 = (_rms(x1) * g2_ref[...]) * (1.0 + sc2_ref[0]) + sh2_ref[0]
    h2_ref[0] = h2.astype(BF16)


def _merge(x, mixed, att, mods, g1, g2, wpool, watt, wgate, bgate, wout, tm):
    b, s, d = x.shape
    pool_dim = mixed.shape[-1]
    row = lambda i, j: (i, j, 0)
    per_b = lambda i, j: (i, 0, 0)
    const = lambda i, j: (0, 0)
    sc1, sh1, gt1, sc2, sh2 = mods
    return pl.pallas_call(
        functools.partial(_merge_kernel, d=d),
        grid=(b, s // tm),
        in_specs=[pl.BlockSpec((1, tm, d), row),
                  pl.BlockSpec((1, tm, pool_dim), row),
                  pl.BlockSpec((1, tm, d), row)]
                 + [pl.BlockSpec((1, 1, d), per_b)] * 5
                 + [pl.BlockSpec((1, d), const), pl.BlockSpec((1, d), const),
                    pl.BlockSpec((pool_dim, d), const), pl.BlockSpec((d, d), const),
                    pl.BlockSpec((d, 2 * d), const), pl.BlockSpec((1, 2 * d), const),
                    pl.BlockSpec((d, d), const)],
        out_specs=[pl.BlockSpec((1, tm, d), row), pl.BlockSpec((1, tm, d), row)],
        out_shape=[jax.ShapeDtypeStruct((b, s, d), F32), jax.ShapeDtypeStruct((b, s, d), BF16)],
        compiler_params=_cparams(("parallel", "parallel")),
        name="merge",
    )(x, mixed, att, sc1, sh1, gt1, sc2, sh2, g1, g2, wpool, watt, wgate, bgate, wout)


ROUTE_TOKENS = 8 * LANES
LAY_PITCH = LANES + 8


def _sort16_network():
    n, pairs, p = TOPK, [], 1
    while p < n:
        k = p
        while k >= 1:
            for j in range(k % p, n - k, 2 * k):
                for i in range(min(k, n - j - k)):
                    if (i + j) // (2 * p) == (i + j + k) // (2 * p):
                        pairs.append((i + j, i + j + k))
            k //= 2
        p *= 2
    return tuple(pairs)


SORT16 = _sort16_network()


def _swap_mask(a, b):
    (av, ai), (bv, bi) = a, b
    return (bv > av) | ((bv == av) & (bi < ai))


def _order(a, b):
    sw = _swap_mask(a, b)
    first = (jnp.maximum(a[0], b[0]), jnp.where(sw, b[1], a[1]))
    second = (jnp.minimum(a[0], b[0]), jnp.where(sw, a[1], b[1]))
    return first, second


def _first_of(a, b):
    return jnp.maximum(a[0], b[0]), jnp.where(_swap_mask(a, b), b[1], a[1])


def _sort16(items):
    items = list(items)
    for i, j in SORT16:
        items[i], items[j] = _order(items[i], items[j])
    return items


def _merge_top16(a, b):
    c = [_first_of(a[i], b[TOPK - 1 - i]) for i in range(TOPK)]
    stride = TOPK // 2
    while stride:
        for i in range(TOPK):
            if not i & stride:
                c[i], c[i + stride] = _order(c[i], c[i + stride])
        stride //= 2
    return c


def _route_kernel(h2_ref, wqt_ref, keys_ref, e1_ref, e2_ref, gate_ref,
                  qt_scr, lay_scr, tv_scr, ti_scr):
    nblk = ROUTE_TOKENS // LANES
    qt = lax.dot_general(wqt_ref[...], h2_ref[...], (((1,), (1,)), ((), ())),
                         preferred_element_type=F32)
    qt_scr[...] = qt.astype(BF16)

    def set_body(si, carry):
        r0 = pl.multiple_of(si * HEAD_DIM, HEAD_DIM)
        st = jnp.dot(keys_ref[si], qt_scr[pl.ds(r0, HEAD_DIM), :], preferred_element_type=F32)
        for c in range(nblk):
            lay_scr[c * LAY_PITCH:c * LAY_PITCH + LANES, :] = st[:, c * LANES:(c + 1) * LANES]
        lists = []
        for g in range(LANES // TOPK):
            items = []
            for k in range(g * TOPK, (g + 1) * TOPK):
                items.append((lay_scr[pl.ds(k, nblk, stride=LAY_PITCH), :],
                              jnp.full((nblk, LANES), k, I32)))
            lists.append(_sort16(items))
        while len(lists) > 1:
            lists = [_merge_top16(lists[2 * i], lists[2 * i + 1]) for i in range(len(lists) // 2)]
        for a, (v, idx) in enumerate(lists[0]):
            tv_scr[si * TOPK + a] = v
            ti_scr[si * TOPK + a] = idx
        return carry

    lax.fori_loop(0, 2 * N_HEADS, set_body, 0)

    def head_body(h, carry):
        b1 = 2 * h * TOPK
        b2 = b1 + TOPK
        cv, ce = [], []
        for a, nb in CAND_ROWS:
            va = tv_scr[b1 + a]
            ea = ti_scr[b1 + a] * LANES
            for b in range(nb):
                cv.append(va + tv_scr[b2 + b])
                ce.append(ea + ti_scr[b2 + b])
        best_s, best_e = [], []
        for _ in range(TOPK):
            nodes = list(zip(cv, ce))
            while len(nodes) > 1:
                nxt = []
                for i in range(0, len(nodes) - 1, 2):
                    (lv, le), (rv, re) = nodes[i], nodes[i + 1]
                    nxt.append((jnp.maximum(lv, rv), jnp.where(rv > lv, re, le)))
                if len(nodes) % 2:
                    nxt.append(nodes[-1])
                nodes = nxt
            m, e = nodes[0]
            best_s.append(m)
            best_e.append(e)
            cv = [jnp.where(c_e == e, NEG_INF, c_v) for c_v, c_e in zip(cv, ce)]
        ex = [jnp.exp(b - best_s[0]) for b in best_s]
        z = ex[0]
        for t in ex[1:]:
            z = z + t
        for k in range(TOPK):
            e1_ref[h * TOPK + k] = lax.shift_right_logical(best_e[k], 7)
            e2_ref[h * TOPK + k] = best_e[k] & (LANES - 1)
            gate_ref[h * TOPK + k] = ex[k] / z
        return carry

    lax.fori_loop(0, N_HEADS, head_body, 0)


def _route(h2_flat, wqt_bf, keys_bf):
    t, d = h2_flat.shape
    assert t % ROUTE_TOKENS == 0
    nslot = N_HEADS * TOPK
    nblk = ROUTE_TOKENS // LANES
    keys2 = keys_bf.reshape(2 * N_HEADS, LANES, HEAD_DIM)
    out = pl.BlockSpec((nslot, nblk, LANES), lambda i: (0, i, 0))
    outs = pl.pallas_call(
        _route_kernel,
        grid=(t // ROUTE_TOKENS,),
        in_specs=[pl.BlockSpec((ROUTE_TOKENS, d), lambda i: (i, 0)),
                  pl.BlockSpec(wqt_bf.shape, lambda i: (0, 0)),
                  pl.BlockSpec(keys2.shape, lambda i: (0, 0, 0))],
        out_specs=[out, out, out],
        out_shape=[jax.ShapeDtypeStruct((nslot, t // LANES, LANES), I32),
                   jax.ShapeDtypeStruct((nslot, t // LANES, LANES), I32),
                   jax.ShapeDtypeStruct((nslot, t // LANES, LANES), F32)],
        scratch_shapes=[pltpu.VMEM((wqt_bf.shape[0], ROUTE_TOKENS), BF16),
                        pltpu.VMEM((nblk * LAY_PITCH, LANES), F32),
                        pltpu.VMEM((2 * N_HEADS * TOPK, nblk, LANES), F32),
                        pltpu.VMEM((2 * N_HEADS * TOPK, nblk, LANES), I32)],
        compiler_params=_cparams(("parallel",)),
        name="route",
    )(h2_flat, wqt_bf, keys2)
    return [o.reshape(nslot, t).T for o in outs]


GATE_ROWS = LANES // 2
GATE_PITCH = GATE_ROWS + 4
SUBLANES = 8
BUILD_GROUP = 64
HI_MASK = -65536


def _peer_kernel(h2_ref, e1_ref, e2_ref, gate_ref, ut_ref, v_ref, x1_ref, gt2_ref, gf_ref,
                 y_ref, gmat_scr, w_scr, acc_scr, *, tp, ec):
    c = pl.program_id(1)
    nsub = ec // LANES

    @pl.when(c == 0)
    def _build_gate_matrices():
        acc_scr[...] = jnp.zeros_like(acc_scr)
        key_id = lax.broadcasted_iota(I32, (LANES, LANES), 0)

        def body(tb, carry):
            for sub in range(BUILD_GROUP // SUBLANES):
                base = pl.multiple_of(tb * BUILD_GROUP + sub * SUBLANES, SUBLANES)
                r1 = e1_ref[pl.ds(base, SUBLANES), :]
                r2 = e2_ref[pl.ds(base, SUBLANES), :]
                gg = gate_ref[pl.ds(base, SUBLANES), :]
                for i in range(SUBLANES):
                    a = jnp.where(key_id == r1[i:i + 1, :], gg[i:i + 1, :], 0.0).astype(BF16)
                    b = jnp.where(key_id == r2[i:i + 1, :], 1.0, 0.0).astype(BF16)
                    gm = lax.dot_general(a, b, (((1,), (1,)), ((), ())),
                                         preferred_element_type=F32)
                    bits = lax.bitcast_convert_type(gm, I32) + 32768
                    packed = ((bits[:GATE_ROWS] & HI_MASK)
                              | lax.shift_right_logical(bits[GATE_ROWS:], 16))
                    gmat_scr[pl.ds((base + i) * GATE_PITCH, GATE_ROWS), :] = packed
            return carry

        lax.fori_loop(0, tp // BUILD_GROUP, body, 0)

    s = jnp.dot(h2_ref[...], ut_ref[...], preferred_element_type=F32)
    for j in range(nsub):
        k1 = c * nsub + j
        word = gmat_scr[pl.ds(k1 % GATE_ROWS, tp, stride=GATE_PITCH), :]
        shift = (k1 // GATE_ROWS) * 16
        gi = lax.bitcast_convert_type(lax.shift_left(word, shift) & HI_MASK, F32)
        sj = s[:, j * LANES:(j + 1) * LANES]
        w = (0.5 * sj) * (1.0 + lax.erf(sj * INV_SQRT2)) * gi
        w_scr[:, j * LANES:(j + 1) * LANES] = w.astype(BF16)
    acc_scr[...] += jnp.dot(w_scr[...], v_ref[...], preferred_element_type=F32)

    @pl.when(c == pl.num_programs(1) - 1)
    def _finish():
        xo = x1_ref[...] + gt2_ref[0] * acc_scr[...]
        y_ref[...] = _rms(xo) * gf_ref[...]


def _peer(h2_flat, e1, e2, gate, ut_bf, v_bf, x1_flat, gt2, g_final, tokens_per_batch, tp, ec):
    t, d = h2_flat.shape
    ne = v_bf.shape[0]
    nslot = e1.shape[1]
    tiles_per_batch = tokens_per_batch // tp
    tok = lambda i, c: (i, 0)
    return pl.pallas_call(
        functools.partial(_peer_kernel, tp=tp, ec=ec),
        grid=(t // tp, ne // ec),
        in_specs=[pl.BlockSpec((tp, d), tok),
                  pl.BlockSpec((tp, nslot), tok),
                  pl.BlockSpec((tp, nslot), tok),
                  pl.BlockSpec((tp, nslot), tok),
                  pl.BlockSpec((d, ec), lambda i, c: (0, c)),
                  pl.BlockSpec((ec, d), lambda i, c: (c, 0)),
                  pl.BlockSpec((tp, d), tok),
                  pl.BlockSpec((1, 1, d), lambda i, c: (i // tiles_per_batch, 0, 0)),
                  pl.BlockSpec((1, d), lambda i, c: (0, 0))],
        out_specs=pl.BlockSpec((tp, d), tok),
        out_shape=jax.ShapeDtypeStruct((t, d), F32),
        scratch_shapes=[pltpu.VMEM((tp * GATE_PITCH, LANES), I32),
                        pltpu.VMEM((tp, ec), BF16),
                        pltpu.VMEM((tp, d), F32)],
        compiler_params=_cparams(("parallel", "arbitrary")),
        name="peer",
    )(h2_flat, e1, e2, gate, ut_bf, v_bf, x1_flat, gt2, g_final)


def _rope_tables(s):
    inv_freq = 1.0 / (ROPE_THETA ** (jnp.arange(0, HEAD_DIM, 2, dtype=F32) / HEAD_DIM))
    ang = jnp.arange(s, dtype=F32)[:, None] * inv_freq[None, :]
    ang = jnp.concatenate([ang, ang, ang, ang], axis=-1)
    cos, sin = jnp.cos(ang), jnp.sin(ang)
    low = (jnp.arange(LANES) % HEAD_DIM) < (HEAD_DIM // 2)
    return cos, jnp.where(low, -sin, 0.0), jnp.where(low, 0.0, sin)


def _pick_tile(n, want):
    t = min(n, want)
    while n % t:
        t //= 2
    return t


def _trunk(x, mod, wts, lam_init):
    b, s, d = x.shape
    sh1, sc1, gt1, sh2, sc2, gt2 = [m.reshape(b, 1, d) for m in jnp.split(mod, 6, axis=-1)]
    cos, sa, sb = _rope_tables(s)
    p, q, k, v = _front(x, sc1, sh1, wts["g_norm1"], wts["w_in"], cos, sa, sb, _pick_tile(s, 512))
    mixed = _pool(p, wts["pool_w"], wts["pool_scale"])
    att = _attention(q, k, v, wts["lamv"], wts["g_sub"], lam_init, _pick_tile(s, 512),
                     _pick_tile(s, 512))
    x1, h2 = _merge(x, mixed, att, (sc1, sh1, gt1, sc2, sh2), wts["g_norm1"], wts["g_norm2"],
                    wts["w_pool_out"], wts["w_att_out"], wts["w_gate"], wts["b_gate"],
                    wts["w_out"], _pick_tile(s, 512))
    t = b * s
    h2f = h2.reshape(t, d)
    e1, e2, gate = _route(h2f, wts["w_peer_qt"], wts["peer_keys"])
    y = _peer(h2f, e1, e2, gate, wts["peer_ut"], wts["peer_v"], x1.reshape(t, d),
              gt2, wts["g_final"], s, _pick_tile(s, 512), 1024)
    return y.reshape(b, s, d)


def kernel(x_prompt, x_sample, c_prompt, c_sample, w_ada, b_ada, g_norm1, w_in, pool_w, pool_scale,
           w_pool_out, lam_q1, lam_k1, lam_q2, lam_k2, g_sub, w_att_out, w_gate, b_gate, w_out,
           g_norm2, w_peer_q, peer_keys, peer_u, peer_v, g_final):
    depth = w_ada.shape[0]
    assert depth == 1, "single-layer trunk"
    l = 0
    lam_init = 0.8 - 0.6 * math.exp(-0.3 * l)
    d = x_prompt.shape[-1]
    wts = {
        "g_norm1": g_norm1[l][None, :], "g_norm2": g_norm2[l][None, :], "g_final": g_final[None, :],
        "w_in": w_in[l].astype(BF16),
        "pool_w": pool_w[l].astype(BF16), "pool_scale": pool_scale[l][None, :],
        "w_pool_out": w_pool_out[l].astype(BF16),
        "lamv": jnp.stack([lam_q1[l], lam_k1[l], lam_q2[l], lam_k2[l]]).astype(F32),
        "g_sub": g_sub[l][None, :],
        "w_att_out": w_att_out[l].astype(BF16),
        "w_gate": w_gate[l].astype(BF16), "b_gate": b_gate[l][None, :],
        "w_out": w_out[l].astype(BF16),
        "w_peer_qt": w_peer_q[l].T.astype(BF16),
        "peer_keys": peer_keys[l].astype(BF16),
        "peer_ut": peer_u[l].T.astype(BF16),
        "peer_v": peer_v[l].astype(BF16),
    }
    nb = x_prompt.shape[0]
    mod = _ada(jnp.concatenate([c_prompt, c_sample], axis=0), w_ada[l].astype(BF16), b_ada[l][None, :])
    y_prompt = _trunk(x_prompt, mod[:nb], wts, lam_init)
    y_sample = _trunk(x_sample, mod[nb:], wts, lam_init)
    return (y_prompt, y_sample)
```

```python
import functools
import math

import jax
import jax.numpy as jnp
from jax import lax
from jax.experimental import pallas as pl
from jax.experimental.pallas import tpu as pltpu

F32 = jnp.float32
BF16 = jnp.bfloat16
I32 = jnp.int32

NORM_EPS = 1e-6
ROPE_THETA = 10000.0
POOL_WINDOWS = (2, 4, 8, 16)
POOL_PAD = 16
N_HEADS = 8
HEAD_DIM = 64
LANES = 128
MXU_DEPTH = 256
TOPK = 16
CAND_ROWS = tuple((a, TOPK // (a + 1)) for a in range(TOPK))
VMEM_LIMIT = 56 * 1024 * 1024
INV_SQRT2 = 1.0 / math.sqrt(2.0)
LOG2E = 1.0 / math.log(2.0)
NEG_INF = float("-inf")


def _cparams(sem):
    return pltpu.CompilerParams(dimension_semantics=sem, vmem_limit_bytes=VMEM_LIMIT)


def _rms(x):
    return x * lax.rsqrt(jnp.mean(x * x, axis=-1, keepdims=True) + NORM_EPS)


def _ada_kernel(c_ref, w_ref, b_ref, o_ref):
    c = c_ref[...]
    a = (c * jax.nn.sigmoid(c)).astype(BF16)
    o_ref[...] = jnp.dot(a, w_ref[...], preferred_element_type=F32) + b_ref[...]


def _ada(c, w_bf, b):
    n, d = c.shape
    nout = w_bf.shape[1]
    return pl.pallas_call(
        _ada_kernel,
        grid=(nout // d,),
        in_specs=[pl.BlockSpec((n, d), lambda j: (0, 0)),
                  pl.BlockSpec((d, d), lambda j: (0, j)),
                  pl.BlockSpec((1, d), lambda j: (0, j))],
        out_specs=pl.BlockSpec((n, d), lambda j: (0, j)),
        out_shape=jax.ShapeDtypeStruct((n, nout), F32),
        compiler_params=_cparams(("parallel",)),
        name="ada",
    )(c, w_bf, b)


def _front_kernel(x_ref, sc_ref, sh_ref, g_ref, w_ref, cos_ref, sa_ref, sb_ref,
                  p_ref, q_ref, k_ref, v_ref, *, d, pool_dim):
    x = x_ref[0]
    h = (_rms(x) * g_ref[...]) * (1.0 + sc_ref[0]) + sh_ref[0]
    proj = jnp.dot(h.astype(BF16), w_ref[...], preferred_element_type=F32)
    p_ref[0] = proj[:, :pool_dim]
    cos, sa, sb = cos_ref[...], sa_ref[...], sb_ref[...]
    half = HEAD_DIM // 2

    def rope(t):
        return (t * cos + pltpu.roll(t, LANES - half, axis=1) * sa
                + pltpu.roll(t, half, axis=1) * sb)

    qscale = HEAD_DIM ** -0.5 * LOG2E
    for j in range(d // LANES):
        o = pool_dim + j * LANES
        q_ref[0, :, j * LANES:(j + 1) * LANES] = (rope(proj[:, o:o + LANES]) * qscale).astype(BF16)
        o = pool_dim + d + j * LANES
        k_ref[0, :, j * LANES:(j + 1) * LANES] = rope(proj[:, o:o + LANES]).astype(BF16)
    v_ref[0] = proj[:, pool_dim + 2 * d:].astype(BF16)


def _front(x, sc, sh, g, w_in_bf, cos, sa, sb, tm):
    b, s, d = x.shape
    in_dim = w_in_bf.shape[1]
    pool_dim = in_dim - 3 * d
    row = lambda i, j: (i, j, 0)
    per_b = lambda i, j: (i, 0, 0)
    const = lambda i, j: (0, 0)
    return pl.pallas_call(
        functools.partial(_front_kernel, d=d, pool_dim=pool_dim),
        grid=(b, s // tm),
        in_specs=[pl.BlockSpec((1, tm, d), row),
                  pl.BlockSpec((1, 1, d), per_b),
                  pl.BlockSpec((1, 1, d), per_b),
                  pl.BlockSpec((1, d), const),
                  pl.BlockSpec((d, in_dim), const),
                  pl.BlockSpec((tm, LANES), lambda i, j: (j, 0)),
                  pl.BlockSpec((tm, LANES), lambda i, j: (j, 0)),
                  pl.BlockSpec((tm, LANES), lambda i, j: (j, 0))],
        out_specs=[pl.BlockSpec((1, tm, pool_dim), row),
                   pl.BlockSpec((1, tm, d), row),
                   pl.BlockSpec((1, tm, d), row),
                   pl.BlockSpec((1, tm, d), row)],
        out_shape=[jax.ShapeDtypeStruct((b, s, pool_dim), F32),
                   jax.ShapeDtypeStruct((b, s, d), BF16),
                   jax.ShapeDtypeStruct((b, s, d), BF16),
                   jax.ShapeDtypeStruct((b, s, d), BF16)],
        compiler_params=_cparams(("parallel", "parallel")),
        name="front",
    )(x, sc, sh, g, w_in_bf, cos, sa, sb)


def _pool_kernel(p_ref, w_ref, scale_ref, o_ref, pad_ref, *, s):
    pos = lax.broadcasted_iota(I32, (s, LANES), 0)
    zeros = jnp.zeros((POOL_PAD, LANES), F32)
    for g, win in enumerate(POOL_WINDOWS):
        half = win // 2
        xg = p_ref[0, :, g * LANES:(g + 1) * LANES]
        pad_ref[0:POOL_PAD, :] = zeros
        pad_ref[POOL_PAD:POOL_PAD + s, :] = xg
        pad_ref[POOL_PAD + s:2 * POOL_PAD + s, :] = zeros
        wsum = pad_ref[POOL_PAD - half:POOL_PAD - half + s, :]
        for dlt in range(-half + 1, half):
            wsum = wsum + pad_ref[POOL_PAD + dlt:POOL_PAD + dlt + s, :]
        cnt = (jnp.minimum(pos + half, s) - jnp.maximum(pos - half, 0)).astype(F32)
        diff = wsum / cnt - xg
        mixed = jnp.dot(diff.astype(BF16), w_ref[g], preferred_element_type=F32)
        o_ref[0, :, g * LANES:(g + 1) * LANES] = (
            mixed * scale_ref[:, g * LANES:(g + 1) * LANES]).astype(BF16)


def _pool(p, pool_w_bf, pool_scale):
    b, s, pool_dim = p.shape
    ng = pool_w_bf.shape[0]
    return pl.pallas_call(
        functools.partial(_pool_kernel, s=s),
        grid=(b,),
        in_specs=[pl.BlockSpec((1, s, pool_dim), lambda i: (i, 0, 0)),
                  pl.BlockSpec((ng, LANES, LANES), lambda i: (0, 0, 0)),
                  pl.BlockSpec((1, pool_dim), lambda i: (0, 0))],
        out_specs=pl.BlockSpec((1, s, pool_dim), lambda i: (i, 0, 0)),
        out_shape=jax.ShapeDtypeStruct((b, s, pool_dim), BF16),
        scratch_shapes=[pltpu.VMEM((s + 2 * POOL_PAD, LANES), F32)],
        compiler_params=_cparams(("parallel",)),
        name="pool",
    )(p, pool_w_bf, pool_scale)


def _attn_kernel(lamv_ref, q_ref, k_ref, v_ref, gsub_ref, o_ref, s_scr, vext_scr,
                 *, lam_init, s, tq, tk):
    lv = lamv_ref[...]
    lam = (jnp.exp(jnp.sum(lv[0:1] * lv[1:2], axis=-1, keepdims=True))
           - jnp.exp(jnp.sum(lv[2:3] * lv[3:4], axis=-1, keepdims=True)) + lam_init)
    @pl.when(pl.program_id(2) == 0)
    def _extend_values():
        vext_scr[:, :LANES] = v_ref[0]
        vext_scr[:, LANES:] = jnp.ones((s, LANES), BF16)

    q = q_ref[0]
    first = lax.broadcasted_iota(I32, (1, LANES), 1) < HEAD_DIM
    zero = jnp.zeros_like(q)
    qs = (jnp.where(first, q, zero), jnp.where(first, zero, q))
    nt = (((1,), (1,)), ((), ()))
    nchunk = s // tk
    mx = [jnp.full((tq, LANES), NEG_INF, F32), jnp.full((tq, LANES), NEG_INF, F32)]
    for c in range(nchunk):
        kc = k_ref[0, c * tk:(c + 1) * tk, :]
        for u in range(2):
            sc = lax.dot_general(qs[u], kc, nt, preferred_element_type=F32)
            s_scr[u, :, c * tk:(c + 1) * tk] = sc
            for jj in range(tk // LANES):
                mx[u] = jnp.maximum(mx[u], sc[:, jj * LANES:(jj + 1) * LANES])
    mx = [jnp.max(m, axis=-1, keepdims=True) for m in mx]
    acc = [jnp.zeros((tq, 2 * LANES), F32), jnp.zeros((tq, 2 * LANES), F32)]
    for c in range(nchunk):
        vc = vext_scr[c * tk:(c + 1) * tk, :]
        for u in range(2):
            p = jnp.exp2(s_scr[u, :, c * tk:(c + 1) * tk] - mx[u]).astype(BF16)
            acc[u] = acc[u] + jnp.dot(p, vc, preferred_element_type=F32)
    c1 = 1.0 / acc[0][:, LANES:LANES + 1]
    c2 = lam / acc[1][:, LANES:LANES + 1]
    o = acc[0][:, :LANES] * c1 - acc[1][:, :LANES] * c2
    o_ref[0] = ((_rms(o) * gsub_ref[...]) * (1.0 - lam_init)).astype(BF16)


def _attention(q, k, v, lamv, g_sub, lam_init, tq, tk):
    b, s, d = q.shape
    nh = d // LANES
    return pl.pallas_call(
        functools.partial(_attn_kernel, lam_init=lam_init, s=s, tq=tq, tk=tk),
        grid=(b, nh, s // tq),
        in_specs=[pl.BlockSpec((4, HEAD_DIM), lambda i, h, j: (0, 0)),
                  pl.BlockSpec((1, tq, LANES), lambda i, h, j: (i, j, h)),
                  pl.BlockSpec((1, s, LANES), lambda i, h, j: (i, 0, h)),
                  pl.BlockSpec((1, s, LANES), lambda i, h, j: (i, 0, h)),
                  pl.BlockSpec((1, LANES), lambda i, h, j: (0, 0))],
        out_specs=pl.BlockSpec((1, tq, LANES), lambda i, h, j: (i, j, h)),
        out_shape=jax.ShapeDtypeStruct((b, s, d), BF16),
        scratch_shapes=[pltpu.VMEM((2, tq, s), F32), pltpu.VMEM((s, 2 * LANES), BF16)],
        compiler_params=_cparams(("parallel", "parallel", "arbitrary")),
        name="attn",
    )(lamv, q, k, v, g_sub)


def _merge_kernel(x_ref, mixed_ref, att_ref, sc1_ref, sh1_ref, gt1_ref, sc2_ref, sh2_ref,
                  g1_ref, g2_ref, wpool_ref, watt_ref, wgate_ref, bgate_ref, wout_ref,
                  x1_ref, h2_ref, *, d):
    x = x_ref[0]
    h = ((_rms(x) * g1_ref[...]) * (1.0 + sc1_ref[0]) + sh1_ref[0]).astype(BF16)
    gates = jax.nn.sigmoid(jnp.dot(h, wgate_ref[...], preferred_element_type=F32) + bgate_ref[...])
    y_pool = jnp.dot(mixed_ref[0], wpool_ref[...], preferred_element_type=F32)
    y_att = jnp.dot(att_ref[0], watt_ref[...], preferred_element_type=F32)
    merged = gates[:, :d] * y_pool + gates[:, d:] * y_att
    x1 = x + gt1_ref[0] * jnp.dot(merged.astype(BF16), wout_ref[...], preferred_element_type=F32)
    x1_ref[0] = x1
    h2 = (_rms(x1) * g2_ref[...]) * (1.0 + sc2_ref[0]) + sh2_ref[0]
    h2_ref[0] = h2.astype(BF16)


def _merge(x, mixed, att, mods, g1, g2, wpool, watt, wgate, bgate, wout, tm):
    b, s, d = x.shape
    pool_dim = mixed.shape[-1]
    row = lambda i, j: (i, j, 0)
    per_b = lambda i, j: (i, 0, 0)
    const = lambda i, j: (0, 0)
    sc1, sh1, gt1, sc2, sh2 = mods
    return pl.pallas_call(
        functools.partial(_merge_kernel, d=d),
        grid=(b, s // tm),
        in_specs=[pl.BlockSpec((1, tm, d), row),
                  pl.BlockSpec((1, tm, pool_dim), row),
                  pl.BlockSpec((1, tm, d), row)]
                 + [pl.BlockSpec((1, 1, d), per_b)] * 5
                 + [pl.BlockSpec((1, d), const), pl.BlockSpec((1, d), const),
                    pl.BlockSpec((pool_dim, d), const), pl.BlockSpec((d, d), const),
                    pl.BlockSpec((d, 2 * d), const), pl.BlockSpec((1, 2 * d), const),
                    pl.BlockSpec((d, d), const)],
        out_specs=[pl.BlockSpec((1, tm, d), row), pl.BlockSpec((1, tm, d), row)],
        out_shape=[jax.ShapeDtypeStruct((b, s, d), F32), jax.ShapeDtypeStruct((b, s, d), BF16)],
        compiler_params=_cparams(("parallel", "parallel")),
        name="merge",
    )(x, mixed, att, sc1, sh1, gt1, sc2, sh2, g1, g2, wpool, watt, wgate, bgate, wout)


ROUTE_TOKENS = 8 * LANES
LAY_PITCH = LANES + 8


def _sort16_network():
    n, pairs, p = TOPK, [], 1
    while p < n:
        k = p
        while k >= 1:
            for j in range(k % p, n - k, 2 * k):
                for i in range(min(k, n - j - k)):
                    if (i + j) // (2 * p) == (i + j + k) // (2 * p):
                        pairs.append((i + j, i + j + k))
            k //= 2
        p *= 2
    return tuple(pairs)


SORT16 = _sort16_network()


def _swap_mask(a, b):
    (av, ai), (bv, bi) = a, b
    return (bv > av) | ((bv == av) & (bi < ai))


def _order(a, b):
    sw = _swap_mask(a, b)
    first = (jnp.maximum(a[0], b[0]), jnp.where(sw, b[1], a[1]))
    second = (jnp.minimum(a[0], b[0]), jnp.where(sw, a[1], b[1]))
    return first, second


def _first_of(a, b):
    return jnp.maximum(a[0], b[0]), jnp.where(_swap_mask(a, b), b[1], a[1])


def _sort16(items):
    items = list(items)
    for i, j in SORT16:
        items[i], items[j] = _order(items[i], items[j])
    return items


def _merge_top16(a, b):
    c = [_first_of(a[i], b[TOPK - 1 - i]) for i in range(TOPK)]
    stride = TOPK // 2
    while stride:
        for i in range(TOPK):
            if not i & stride:
                c[i], c[i + stride] = _order(c[i], c[i + stride])
        stride //= 2
    return c


def _route_kernel(h2_ref, wqt_ref, keys_ref, e1_ref, e2_ref, gate_ref,
                  qt_scr, lay_scr, tv_scr, ti_scr):
    nblk = ROUTE_TOKENS // LANES
    qt = lax.dot_general(wqt_ref[...], h2_ref[...], (((1,), (1,)), ((), ())),
                         preferred_element_type=F32)
    qt_scr[...] = qt.astype(BF16)

    def set_body(si, carry):
        r0 = pl.multiple_of(si * HEAD_DIM, HEAD_DIM)
        st = jnp.dot(keys_ref[si], qt_scr[pl.ds(r0, HEAD_DIM), :], preferred_element_type=F32)
        for c in range(nblk):
            lay_scr[c * LAY_PITCH:c * LAY_PITCH + LANES, :] = st[:, c * LANES:(c + 1) * LANES]
        lists = []
        for g in range(LANES // TOPK):
            items = []
            for k in range(g * TOPK, (g + 1) * TOPK):
                items.append((lay_scr[pl.ds(k, nblk, stride=LAY_PITCH), :],
                              jnp.full((nblk, LANES), k, I32)))
            lists.append(_sort16(items))
        while len(lists) > 1:
            lists = [_merge_top16(lists[2 * i], lists[2 * i + 1]) for i in range(len(lists) // 2)]
        for a, (v, idx) in enumerate(lists[0]):
            tv_scr[si * TOPK + a] = v
            ti_scr[si * TOPK + a] = idx
        return carry

    lax.fori_loop(0, 2 * N_HEADS, set_body, 0)

    def head_body(h, carry):
        b1 = 2 * h * TOPK
        b2 = b1 + TOPK
        cv, ce = [], []
        for a, nb in CAND_ROWS:
            va = tv_scr[b1 + a]
            ea = ti_scr[b1 + a] * LANES
            for b in range(nb):
                cv.append(va + tv_scr[b2 + b])
                ce.append(ea + ti_scr[b2 + b])
        best_s, best_e = [], []
        for _ in range(TOPK):
            nodes = list(zip(cv, ce))
            while len(nodes) > 1:
                nxt = []
                for i in range(0, len(nodes) - 1, 2):
                    (lv, le), (rv, re) = nodes[i], nodes[i + 1]
                    nxt.append((jnp.maximum(lv, rv), jnp.where(rv > lv, re, le)))
                if len(nodes) % 2:
                    nxt.append(nodes[-1])
                nodes = nxt
            m, e = nodes[0]
            best_s.append(m)
            best_e.append(e)
            cv = [jnp.where(c_e == e, NEG_INF, c_v) for c_v, c_e in zip(cv, ce)]
        ex = [jnp.exp(b - best_s[0]) for b in best_s]
        z = ex[0]
        for t in ex[1:]:
            z = z + t
        for k in range(TOPK):
            e1_ref[h * TOPK + k] = lax.shift_right_logical(best_e[k], 7)
            e2_ref[h * TOPK + k] = best_e[k] & (LANES - 1)
            gate_ref[h * TOPK + k] = ex[k] / z
        return carry

    lax.fori_loop(0, N_HEADS, head_body, 0)


def _route(h2_flat, wqt_bf, keys_bf):
    t, d = h2_flat.shape
    assert t % ROUTE_TOKENS == 0
    nslot = N_HEADS * TOPK
    nblk = ROUTE_TOKENS // LANES
    keys2 = keys_bf.reshape(2 * N_HEADS, LANES, HEAD_DIM)
    out = pl.BlockSpec((nslot, nblk, LANES), lambda i: (0, i, 0))
    outs = pl.pallas_call(
        _route_kernel,
        grid=(t // ROUTE_TOKENS,),
        in_specs=[pl.BlockSpec((ROUTE_TOKENS, d), lambda i: (i, 0)),
                  pl.BlockSpec(wqt_bf.shape, lambda i: (0, 0)),
                  pl.BlockSpec(keys2.shape, lambda i: (0, 0, 0))],
        out_specs=[out, out, out],
        out_shape=[jax.ShapeDtypeStruct((nslot, t // LANES, LANES), I32),
                   jax.ShapeDtypeStruct((nslot, t // LANES, LANES), I32),
                   jax.ShapeDtypeStruct((nslot, t // LANES, LANES), F32)],
        scratch_shapes=[pltpu.VMEM((wqt_bf.shape[0], ROUTE_TOKENS), BF16),
                        pltpu.VMEM((nblk * LAY_PITCH, LANES), F32),
                        pltpu.VMEM((2 * N_HEADS * TOPK, nblk, LANES), F32),
                        pltpu.VMEM((2 * N_HEADS * TOPK, nblk, LANES), I32)],
        compiler_params=_cparams(("parallel",)),
        name="route",
    )(h2_flat, wqt_bf, keys2)
    return [o.reshape(nslot, t).T for o in outs]


GATE_ROWS = LANES // 2
GATE_PITCH = GATE_ROWS + 4
SUBLANES = 8
BUILD_GROUP = 64
HI_MASK = -65536


def _peer_kernel(h2_ref, e1_ref, e2_ref, gate_ref, ut_ref, v_ref, x1_ref, gt2_ref, gf_ref,
                 y_ref, gmat_scr, w_scr, acc_scr, *, tp, ec):
    c = pl.program_id(1)
    nsub = ec // LANES

    @pl.when(c == 0)
    def _build_gate_matrices():
        acc_scr[...] = jnp.zeros_like(acc_scr)
        key_id = lax.broadcasted_iota(I32, (LANES, LANES), 0)

        def body(tb, carry):
            for sub in range(BUILD_GROUP // SUBLANES):
                base = pl.multiple_of(tb * BUILD_GROUP + sub * SUBLANES, SUBLANES)
                r1 = e1_ref[pl.ds(base, SUBLANES), :]
                r2 = e2_ref[pl.ds(base, SUBLANES), :]
                gg = gate_ref[pl.ds(base, SUBLANES), :]
                for i in range(SUBLANES):
                    a = jnp.where(key_id == r1[i:i + 1, :], gg[i:i + 1, :], 0.0).astype(BF16)
                    b = jnp.where(key_id == r2[i:i + 1, :], 1.0, 0.0).astype(BF16)
                    gm = lax.dot_general(a, b, (((1,), (1,)), ((), ())),
                                         preferred_element_type=F32)
                    bits = lax.bitcast_convert_type(gm, I32) + 32768
                    packed = ((bits[:GATE_ROWS] & HI_MASK)
                              | lax.shift_right_logical(bits[GATE_ROWS:], 16))
                    gmat_scr[pl.ds((base + i) * GATE_PITCH, GATE_ROWS), :] = packed
            return carry

        lax.fori_loop(0, tp // BUILD_GROUP, body, 0)

    h2 = h2_ref[...]
    for jb in range(ec // MXU_DEPTH):
        lo = jb * MXU_DEPTH
        s = jnp.dot(h2, ut_ref[:, lo:lo + MXU_DEPTH], preferred_element_type=F32)
        for u in range(MXU_DEPTH // LANES):
            k1 = c * nsub + jb * (MXU_DEPTH // LANES) + u
            word = gmat_scr[pl.ds(k1 % GATE_ROWS, tp, stride=GATE_PITCH), :]
            shift = (k1 // GATE_ROWS) * 16
            gi = lax.bitcast_convert_type(lax.shift_left(word, shift) & HI_MASK, F32)
            su = s[:, u * LANES:(u + 1) * LANES]
            w_scr[:, lo + u * LANES:lo + (u + 1) * LANES] = (
                (0.5 * su) * (1.0 + lax.erf(su * INV_SQRT2)) * gi).astype(BF16)
    acc_scr[...] += jnp.dot(w_scr[...], v_ref[...], preferred_element_type=F32)

    @pl.when(c == pl.num_programs(1) - 1)
    def _finish():
        xo = x1_ref[...] + gt2_ref[0] * acc_scr[...]
        y_ref[...] = _rms(xo) * gf_ref[...]


def _peer(h2_flat, e1, e2, gate, ut_bf, v_bf, x1_flat, gt2, g_final, tokens_per_batch, tp, ec):
    t, d = h2_flat.shape
    ne = v_bf.shape[0]
    nslot = e1.shape[1]
    tiles_per_batch = tokens_per_batch // tp
    tok = lambda i, c: (i, 0)
    once = pl.Buffered(1)
    return pl.pallas_call(
        functools.partial(_peer_kernel, tp=tp, ec=ec),
        grid=(t // tp, ne // ec),
        in_specs=[pl.BlockSpec((tp, d), tok, pipeline_mode=once),
                  pl.BlockSpec((tp, nslot), tok, pipeline_mode=once),
                  pl.BlockSpec((tp, nslot), tok, pipeline_mode=once),
                  pl.BlockSpec((tp, nslot), tok, pipeline_mode=once),
                  pl.BlockSpec((d, ec), lambda i, c: (0, c)),
                  pl.BlockSpec((ec, d), lambda i, c: (c, 0)),
                  pl.BlockSpec((tp, d), tok, pipeline_mode=once),
                  pl.BlockSpec((1, 1, d), lambda i, c: (i // tiles_per_batch, 0, 0)),
                  pl.BlockSpec((1, d), lambda i, c: (0, 0))],
        out_specs=pl.BlockSpec((tp, d), tok),
        out_shape=jax.ShapeDtypeStruct((t, d), F32),
        scratch_shapes=[pltpu.VMEM((tp * GATE_PITCH, LANES), I32),
                        pltpu.VMEM((tp, ec), BF16),
                        pltpu.VMEM((tp, d), F32)],
        compiler_params=_cparams(("parallel", "arbitrary")),
        name="peer",
    )(h2_flat, e1, e2, gate, ut_bf, v_bf, x1_flat, gt2, g_final)


def _rope_tables(s):
    inv_freq = 1.0 / (ROPE_THETA ** (jnp.arange(0, HEAD_DIM, 2, dtype=F32) / HEAD_DIM))
    ang = jnp.arange(s, dtype=F32)[:, None] * inv_freq[None, :]
    ang = jnp.concatenate([ang, ang, ang, ang], axis=-1)
    cos, sin = jnp.cos(ang), jnp.sin(ang)
    low = (jnp.arange(LANES) % HEAD_DIM) < (HEAD_DIM // 2)
    return cos, jnp.where(low, -sin, 0.0), jnp.where(low, 0.0, sin)


def _pick_tile(n, want):
    t = min(n, want)
    while n % t:
        t //= 2
    return t


def _trunk(x, mod, wts, lam_init):
    b, s, d = x.shape
    sh1, sc1, gt1, sh2, sc2, gt2 = [m.reshape(b, 1, d) for m in jnp.split(mod, 6, axis=-1)]
    cos, sa, sb = _rope_tables(s)
    p, q, k, v = _front(x, sc1, sh1, wts["g_norm1"], wts["w_in"], cos, sa, sb, _pick_tile(s, 512))
    mixed = _pool(p, wts["pool_w"], wts["pool_scale"])
    att = _attention(q, k, v, wts["lamv"], wts["g_sub"], lam_init, _pick_tile(s, 512),
                     _pick_tile(s, 512))
    x1, h2 = _merge(x, mixed, att, (sc1, sh1, gt1, sc2, sh2), wts["g_norm1"], wts["g_norm2"],
                    wts["w_pool_out"], wts["w_att_out"], wts["w_gate"], wts["b_gate"],
                    wts["w_out"], _pick_tile(s, 512))
    t = b * s
    h2f = h2.reshape(t, d)
    e1, e2, gate = _route(h2f, wts["w_peer_qt"], wts["peer_keys"])
    y = _peer(h2f, e1, e2, gate, wts["peer_ut"], wts["peer_v"], x1.reshape(t, d),
              gt2, wts["g_final"], s, _pick_tile(s, 512), 2048)
    return y.reshape(b, s, d)


def kernel(x_prompt, x_sample, c_prompt, c_sample, w_ada, b_ada, g_norm1, w_in, pool_w, pool_scale,
           w_pool_out, lam_q1, lam_k1, lam_q2, lam_k2, g_sub, w_att_out, w_gate, b_gate, w_out,
           g_norm2, w_peer_q, peer_keys, peer_u, peer_v, g_final):
    depth = w_ada.shape[0]
    assert depth == 1, "single-layer trunk"
    l = 0
    lam_init = 0.8 - 0.6 * math.exp(-0.3 * l)
    d = x_prompt.shape[-1]
    wts = {
        "g_norm1": g_norm1[l][None, :], "g_norm2": g_norm2[l][None, :], "g_final": g_final[None, :],
        "w_in": w_in[l].astype(BF16),
        "pool_w": pool_w[l].astype(BF16), "pool_scale": pool_scale[l][None, :],
        "w_pool_out": w_pool_out[l].astype(BF16),
        "lamv": jnp.stack([lam_q1[l], lam_k1[l], lam_q2[l], lam_k2[l]]).astype(F32),
        "g_sub": g_sub[l][None, :],
        "w_att_out": w_att_out[l].astype(BF16),
        "w_gate": w_gate[l].astype(BF16), "b_gate": b_gate[l][None, :],
        "w_out": w_out[l].astype(BF16),
        "w_peer_qt": w_peer_q[l].T.astype(BF16),
        "peer_keys": peer_keys[l].astype(BF16),
        "peer_ut": peer_u[l].T.astype(BF16),
        "peer_v": peer_v[l].astype(BF16),
    }
    nb = x_prompt.shape[0]
    mod = _ada(jnp.concatenate([c_prompt, c_sample], axis=0), w_ada[l].astype(BF16), b_ada[l][None, :])
    y_prompt = _trunk(x_prompt, mod[:nb], wts, lam_init)
    y_sample = _trunk(x_sample, mod[nb:], wts, lam_init)
    return (y_prompt, y_sample)
```

```python
import functools
import math

import jax
import jax.numpy as jnp
from jax import lax
from jax.experimental import pallas as pl
from jax.experimental.pallas import tpu as pltpu

F32 = jnp.float32
BF16 = jnp.bfloat16
I32 = jnp.int32

NORM_EPS = 1e-6
ROPE_THETA = 10000.0
POOL_WINDOWS = (2, 4, 8, 16)
POOL_PAD = 16
N_HEADS = 8
HEAD_DIM = 64
LANES = 128
MXU_DEPTH = 256
TOPK = 16
CAND_ROWS = tuple((a, TOPK // (a + 1)) for a in range(TOPK))
VMEM_LIMIT = 56 * 1024 * 1024
INV_SQRT2 = 1.0 / math.sqrt(2.0)
LOG2E = 1.0 / math.log(2.0)
NEG_INF = float("-inf")


def _cparams(sem):
    return pltpu.CompilerParams(dimension_semantics=sem, vmem_limit_bytes=VMEM_LIMIT)


def _rms(x):
    return x * lax.rsqrt(jnp.mean(x * x, axis=-1, keepdims=True) + NORM_EPS)


def _ada_kernel(c_ref, w_ref, b_ref, o_ref):
    c = c_ref[...]
    a = (c * jax.nn.sigmoid(c)).astype(BF16)
    o_ref[...] = jnp.dot(a, w_ref[...], preferred_element_type=F32) + b_ref[...]


def _ada(c, w_bf, b):
    n, d = c.shape
    nout = w_bf.shape[1]
    return pl.pallas_call(
        _ada_kernel,
        grid=(nout // d,),
        in_specs=[pl.BlockSpec((n, d), lambda j: (0, 0)),
                  pl.BlockSpec((d, d), lambda j: (0, j)),
                  pl.BlockSpec((1, d), lambda j: (0, j))],
        out_specs=pl.BlockSpec((n, d), lambda j: (0, j)),
        out_shape=jax.ShapeDtypeStruct((n, nout), F32),
        compiler_params=_cparams(("parallel",)),
        name="ada",
    )(c, w_bf, b)


def _front_kernel(x_ref, sc_ref, sh_ref, g_ref, w_ref, cos_ref, sa_ref, sb_ref,
                  p_ref, q_ref, k_ref, v_ref, *, d, pool_dim):
    x = x_ref[0]
    h = (_rms(x) * g_ref[...]) * (1.0 + sc_ref[0]) + sh_ref[0]
    proj = jnp.dot(h.astype(BF16), w_ref[...], preferred_element_type=F32)
    p_ref[0] = proj[:, :pool_dim]
    cos, sa, sb = cos_ref[...], sa_ref[...], sb_ref[...]
    half = HEAD_DIM // 2

    def rope(t):
        return (t * cos + pltpu.roll(t, LANES - half, axis=1) * sa
                + pltpu.roll(t, half, axis=1) * sb)

    qscale = HEAD_DIM ** -0.5 * LOG2E
    for j in range(d // LANES):
        o = pool_dim + j * LANES
        q_ref[0, j] = (rope(proj[:, o:o + LANES]) * qscale).astype(BF16)
        o = pool_dim + d + j * LANES
        k_ref[0, j] = rope(proj[:, o:o + LANES]).astype(BF16)
        o = pool_dim + 2 * d + j * LANES
        v_ref[0, j] = proj[:, o:o + LANES].astype(BF16)


def _front(x, sc, sh, g, w_in_bf, cos, sa, sb, tm):
    b, s, d = x.shape
    in_dim = w_in_bf.shape[1]
    pool_dim = in_dim - 3 * d
    nh = d // LANES
    row = lambda i, j: (i, j, 0)
    per_b = lambda i, j: (i, 0, 0)
    const = lambda i, j: (0, 0)
    return pl.pallas_call(
        functools.partial(_front_kernel, d=d, pool_dim=pool_dim),
        grid=(b, s // tm),
        in_specs=[pl.BlockSpec((1, tm, d), row),
                  pl.BlockSpec((1, 1, d), per_b),
                  pl.BlockSpec((1, 1, d), per_b),
                  pl.BlockSpec((1, d), const),
                  pl.BlockSpec((d, in_dim), const),
                  pl.BlockSpec((tm, LANES), lambda i, j: (j, 0)),
                  pl.BlockSpec((tm, LANES), lambda i, j: (j, 0)),
                  pl.BlockSpec((tm, LANES), lambda i, j: (j, 0))],
        out_specs=[pl.BlockSpec((1, tm, pool_dim), row)]
                  + [pl.BlockSpec((1, nh, tm, LANES), lambda i, j: (i, 0, j, 0))] * 3,
        out_shape=[jax.ShapeDtypeStruct((b, s, pool_dim), F32)]
                  + [jax.ShapeDtypeStruct((b, nh, s, LANES), BF16)] * 3,
        compiler_params=_cparams(("parallel", "parallel")),
        name="front",
    )(x, sc, sh, g, w_in_bf, cos, sa, sb)


def _pool_kernel(p_ref, w_ref, scale_ref, o_ref, pad_ref, *, s):
    pos = lax.broadcasted_iota(I32, (s, LANES), 0)
    zeros = jnp.zeros((POOL_PAD, LANES), F32)
    for g, win in enumerate(POOL_WINDOWS):
        half = win // 2
        xg = p_ref[0, :, g * LANES:(g + 1) * LANES]
        pad_ref[0:POOL_PAD, :] = zeros
        pad_ref[POOL_PAD:POOL_PAD + s, :] = xg
        pad_ref[POOL_PAD + s:2 * POOL_PAD + s, :] = zeros
        wsum = pad_ref[POOL_PAD - half:POOL_PAD - half + s, :]
        for dlt in range(-half + 1, half):
            wsum = wsum + pad_ref[POOL_PAD + dlt:POOL_PAD + dlt + s, :]
        cnt = (jnp.minimum(pos + half, s) - jnp.maximum(pos - half, 0)).astype(F32)
        diff = wsum / cnt - xg
        mixed = jnp.dot(diff.astype(BF16), w_ref[g], preferred_element_type=F32)
        o_ref[0, :, g * LANES:(g + 1) * LANES] = (
            mixed * scale_ref[:, g * LANES:(g + 1) * LANES]).astype(BF16)


def _pool(p, pool_w_bf, pool_scale):
    b, s, pool_dim = p.shape
    ng = pool_w_bf.shape[0]
    return pl.pallas_call(
        functools.partial(_pool_kernel, s=s),
        grid=(b,),
        in_specs=[pl.BlockSpec((1, s, pool_dim), lambda i: (i, 0, 0)),
                  pl.BlockSpec((ng, LANES, LANES), lambda i: (0, 0, 0)),
                  pl.BlockSpec((1, pool_dim), lambda i: (0, 0))],
        out_specs=pl.BlockSpec((1, s, pool_dim), lambda i: (i, 0, 0)),
        out_shape=jax.ShapeDtypeStruct((b, s, pool_dim), BF16),
        scratch_shapes=[pltpu.VMEM((s + 2 * POOL_PAD, LANES), F32)],
        compiler_params=_cparams(("parallel",)),
        name="pool",
    )(p, pool_w_bf, pool_scale)


def _attn_kernel(lamv_ref, q_ref, k_ref, v_ref, gsub_ref, o_ref, s_scr, vext_scr,
                 *, lam_init, s, tq, tk):
    lv = lamv_ref[...]
    lam = (jnp.exp(jnp.sum(lv[0:1] * lv[1:2], axis=-1, keepdims=True))
           - jnp.exp(jnp.sum(lv[2:3] * lv[3:4], axis=-1, keepdims=True)) + lam_init)
    @pl.when(pl.program_id(2) == 0)
    def _extend_values():
        vext_scr[:, :LANES] = v_ref[0, 0]
        vext_scr[:, LANES:] = jnp.ones((s, LANES), BF16)

    q = q_ref[0, 0]
    first = lax.broadcasted_iota(I32, (1, LANES), 1) < HEAD_DIM
    zero = jnp.zeros_like(q)
    qs = (jnp.where(first, q, zero), jnp.where(first, zero, q))
    nt = (((1,), (1,)), ((), ()))
    nchunk = s // tk
    mx = [jnp.full((tq, LANES), NEG_INF, F32), jnp.full((tq, LANES), NEG_INF, F32)]
    for c in range(nchunk):
        kc = k_ref[0, 0, c * tk:(c + 1) * tk, :]
        for u in range(2):
            sc = lax.dot_general(qs[u], kc, nt, preferred_element_type=F32)
            s_scr[u, :, c * tk:(c + 1) * tk] = sc
            for jj in range(tk // LANES):
                mx[u] = jnp.maximum(mx[u], sc[:, jj * LANES:(jj + 1) * LANES])
    mx = [jnp.max(m, axis=-1, keepdims=True) for m in mx]
    acc = [jnp.zeros((tq, 2 * LANES), F32), jnp.zeros((tq, 2 * LANES), F32)]
    for c in range(nchunk):
        vc = vext_scr[c * tk:(c + 1) * tk, :]
        for u in range(2):
            p = jnp.exp2(s_scr[u, :, c * tk:(c + 1) * tk] - mx[u]).astype(BF16)
            acc[u] = acc[u] + jnp.dot(p, vc, preferred_element_type=F32)
    c1 = 1.0 / acc[0][:, LANES:LANES + 1]
    c2 = lam / acc[1][:, LANES:LANES + 1]
    o = acc[0][:, :LANES] * c1 - acc[1][:, :LANES] * c2
    o_ref[0, 0] = ((_rms(o) * gsub_ref[...]) * (1.0 - lam_init)).astype(BF16)


def _attention(q, k, v, lamv, g_sub, lam_init, tq, tk):
    b, nh, s, _ = q.shape
    return pl.pallas_call(
        functools.partial(_attn_kernel, lam_init=lam_init, s=s, tq=tq, tk=tk),
        grid=(b, nh, s // tq),
        in_specs=[pl.BlockSpec((4, HEAD_DIM), lambda i, h, j: (0, 0)),
                  pl.BlockSpec((1, 1, tq, LANES), lambda i, h, j: (i, h, j, 0)),
                  pl.BlockSpec((1, 1, s, LANES), lambda i, h, j: (i, h, 0, 0)),
                  pl.BlockSpec((1, 1, s, LANES), lambda i, h, j: (i, h, 0, 0)),
                  pl.BlockSpec((1, LANES), lambda i, h, j: (0, 0))],
        out_specs=pl.BlockSpec((1, 1, tq, LANES), lambda i, h, j: (i, h, j, 0)),
        out_shape=jax.ShapeDtypeStruct((b, nh, s, LANES), BF16),
        scratch_shapes=[pltpu.VMEM((2, tq, s), F32), pltpu.VMEM((s, 2 * LANES), BF16)],
        compiler_params=_cparams(("parallel", "parallel", "arbitrary")),
        name="attn",
    )(lamv, q, k, v, g_sub)


def _merge_kernel(x_ref, mixed_ref, att_ref, sc1_ref, sh1_ref, gt1_ref, sc2_ref, sh2_ref,
                  g1_ref, g2_ref, wpool_ref, watt_ref, wgate_ref, bgate_ref, wout_ref,
                  x1_ref, h2_ref, *, d):
    x = x_ref[0]
    h = ((_rms(x) * g1_ref[...]) * (1.0 + sc1_ref[0]) + sh1_ref[0]).astype(BF16)
    gates = jax.nn.sigmoid(jnp.dot(h, wgate_ref[...], preferred_element_type=F32) + bgate_ref[...])
    y_pool = jnp.dot(mixed_ref[0], wpool_ref[...], preferred_element_type=F32)
    att = jnp.concatenate([att_ref[0, hd] for hd in range(d // LANES)], axis=1)
    y_att = jnp.dot(att, watt_ref[...], preferred_element_type=F32)
    merged = gates[:, :d] * y_pool + gates[:, d:] * y_att
    x1 = x + gt1_ref[0] * jnp.dot(merged.astype(BF16), wout_ref[...], preferred_element_type=F32)
    x1_ref[0] = x1
    h2 = (_rms(x1) * g2_ref[...]) * (1.0 + sc2_ref[0]) + sh2_ref[0]
    h2_ref[0] = h2.astype(BF16)


def _merge(x, mixed, att, mods, g1, g2, wpool, watt, wgate, bgate, wout, tm):
    b, s, d = x.shape
    pool_dim = mixed.shape[-1]
    row = lambda i, j: (i, j, 0)
    per_b = lambda i, j: (i, 0, 0)
    const = lambda i, j: (0, 0)
    sc1, sh1, gt1, sc2, sh2 = mods
    return pl.pallas_call(
        functools.partial(_merge_kernel, d=d),
        grid=(b, s // tm),
        in_specs=[pl.BlockSpec((1, tm, d), row),
                  pl.BlockSpec((1, tm, pool_dim), row),
                  pl.BlockSpec((1, d // LANES, tm, LANES), lambda i, j: (i, 0, j, 0))]
                 + [pl.BlockSpec((1, 1, d), per_b)] * 5
                 + [pl.BlockSpec((1, d), const), pl.BlockSpec((1, d), const),
                    pl.BlockSpec((pool_dim, d), const), pl.BlockSpec((d, d), const),
                    pl.BlockSpec((d, 2 * d), const), pl.BlockSpec((1, 2 * d), const),
                    pl.BlockSpec((d, d), const)],
        out_specs=[pl.BlockSpec((1, tm, d), row), pl.BlockSpec((1, tm, d), row)],
        out_shape=[jax.ShapeDtypeStruct((b, s, d), F32), jax.ShapeDtypeStruct((b, s, d), BF16)],
        compiler_params=_cparams(("parallel", "parallel")),
        name="merge",
    )(x, mixed, att, sc1, sh1, gt1, sc2, sh2, g1, g2, wpool, watt, wgate, bgate, wout)


ROUTE_TOKENS = 8 * LANES
LAY_PITCH = LANES + 8


def _sort16_network():
    n, pairs, p = TOPK, [], 1
    while p < n:
        k = p
        while k >= 1:
            for j in range(k % p, n - k, 2 * k):
                for i in range(min(k, n - j - k)):
                    if (i + j) // (2 * p) == (i + j + k) // (2 * p):
                        pairs.append((i + j, i + j + k))
            k //= 2
        p *= 2
    return tuple(pairs)


SORT16 = _sort16_network()


def _swap_mask(a, b):
    (av, ai), (bv, bi) = a, b
    return (bv > av) | ((bv == av) & (bi < ai))


def _order(a, b):
    sw = _swap_mask(a, b)
    first = (jnp.maximum(a[0], b[0]), jnp.where(sw, b[1], a[1]))
    second = (jnp.minimum(a[0], b[0]), jnp.where(sw, a[1], b[1]))
    return first, second


def _first_of(a, b):
    return jnp.maximum(a[0], b[0]), jnp.where(_swap_mask(a, b), b[1], a[1])


def _sort16(items):
    items = list(items)
    for i, j in SORT16:
        items[i], items[j] = _order(items[i], items[j])
    return items


def _merge_top16(a, b):
    c = [_first_of(a[i], b[TOPK - 1 - i]) for i in range(TOPK)]
    stride = TOPK // 2
    while stride:
        for i in range(TOPK):
            if not i & stride:
                c[i], c[i + stride] = _order(c[i], c[i + stride])
        stride //= 2
    return c


def _route_kernel(h2_ref, wqt_ref, keys_ref, e1_ref, e2_ref, gate_ref,
                  qt_scr, lay_scr, tv_scr, ti_scr):
    nblk = ROUTE_TOKENS // LANES
    qt = lax.dot_general(wqt_ref[...], h2_ref[...], (((1,), (1,)), ((), ())),
                         preferred_element_type=F32)
    qt_scr[...] = qt.astype(BF16)

    def set_body(si, carry):
        r0 = pl.multiple_of(si * HEAD_DIM, HEAD_DIM)
        st = jnp.dot(keys_ref[si], qt_scr[pl.ds(r0, HEAD_DIM), :], preferred_element_type=F32)
        for c in range(nblk):
            lay_scr[c * LAY_PITCH:c * LAY_PITCH + LANES, :] = st[:, c * LANES:(c + 1) * LANES]
        lists = []
        for g in range(LANES // TOPK):
            items = []
            for k in range(g * TOPK, (g + 1) * TOPK):
                items.append((lay_scr[pl.ds(k, nblk, stride=LAY_PITCH), :],
                              jnp.full((nblk, LANES), k, I32)))
            lists.append(_sort16(items))
        while len(lists) > 1:
            lists = [_merge_top16(lists[2 * i], lists[2 * i + 1]) for i in range(len(lists) // 2)]
        for a, (v, idx) in enumerate(lists[0]):
            tv_scr[si * TOPK + a] = v
            ti_scr[si * TOPK + a] = idx
        return carry

    lax.fori_loop(0, 2 * N_HEADS, set_body, 0)

    def head_body(h, carry):
        b1 = 2 * h * TOPK
        b2 = b1 + TOPK
        cv, ce = [], []
        for a, nb in CAND_ROWS:
            va = tv_scr[b1 + a]
            ea = ti_scr[b1 + a] * LANES
            for b in range(nb):
                cv.append(va + tv_scr[b2 + b])
                ce.append(ea + ti_scr[b2 + b])
        best_s, best_e = [], []
        for _ in range(TOPK):
            nodes = list(zip(cv, ce))
            while len(nodes) > 1:
                nxt = []
                for i in range(0, len(nodes) - 1, 2):
                    (lv, le), (rv, re) = nodes[i], nodes[i + 1]
                    nxt.append((jnp.maximum(lv, rv), jnp.where(rv > lv, re, le)))
                if len(nodes) % 2:
                    nxt.append(nodes[-1])
                nodes = nxt
            m, e = nodes[0]
            best_s.append(m)
            best_e.append(e)
            cv = [jnp.where(c_e == e, NEG_INF, c_v) for c_v, c_e in zip(cv, ce)]
        ex = [jnp.exp(b - best_s[0]) for b in best_s]
        z = ex[0]
        for t in ex[1:]:
            z = z + t
        for k in range(TOPK):
            e1_ref[h * TOPK + k] = lax.shift_right_logical(best_e[k], 7)
            e2_ref[h * TOPK + k] = best_e[k] & (LANES - 1)
            gate_ref[h * TOPK + k] = ex[k] / z
        return carry

    lax.fori_loop(0, N_HEADS, head_body, 0)


def _route(h2_flat, wqt_bf, keys_bf):
    t, d = h2_flat.shape
    assert t % ROUTE_TOKENS == 0
    nslot = N_HEADS * TOPK
    nblk = ROUTE_TOKENS // LANES
    keys2 = keys_bf.reshape(2 * N_HEADS, LANES, HEAD_DIM)
    out = pl.BlockSpec((nslot, nblk, LANES), lambda i: (0, i, 0))
    outs = pl.pallas_call(
        _route_kernel,
        grid=(t // ROUTE_TOKENS,),
        in_specs=[pl.BlockSpec((ROUTE_TOKENS, d), lambda i: (i, 0)),
                  pl.BlockSpec(wqt_bf.shape, lambda i: (0, 0)),
                  pl.BlockSpec(keys2.shape, lambda i: (0, 0, 0))],
        out_specs=[out, out, out],
        out_shape=[jax.ShapeDtypeStruct((nslot, t // LANES, LANES), I32),
                   jax.ShapeDtypeStruct((nslot, t // LANES, LANES), I32),
                   jax.ShapeDtypeStruct((nslot, t // LANES, LANES), F32)],
        scratch_shapes=[pltpu.VMEM((wqt_bf.shape[0], ROUTE_TOKENS), BF16),
                        pltpu.VMEM((nblk * LAY_PITCH, LANES), F32),
                        pltpu.VMEM((2 * N_HEADS * TOPK, nblk, LANES), F32),
                        pltpu.VMEM((2 * N_HEADS * TOPK, nblk, LANES), I32)],
        compiler_params=_cparams(("parallel",)),
        name="route",
    )(h2_flat, wqt_bf, keys2)
    return [o.reshape(nslot, t).T for o in outs]


GATE_ROWS = LANES // 2
GATE_PITCH = GATE_ROWS + 4
SUBLANES = 8
BUILD_GROUP = 64
HI_MASK = -65536
PEER_CHUNK = 2048


def _peer_kernel(h2_ref, e1_ref, e2_ref, gate_ref, ut_ref, v_ref, x1_ref, gt2_ref, gf_ref,
                 y_ref, gmat_scr, w_scr, acc_scr, *, tp, ec):
    c = pl.program_id(1)
    nsub = ec // LANES

    @pl.when(c == 0)
    def _build_gate_matrices():
        acc_scr[...] = jnp.zeros_like(acc_scr)
        key_id = lax.broadcasted_iota(I32, (LANES, LANES), 0)

        def body(tb, carry):
            for sub in range(BUILD_GROUP // SUBLANES):
                base = pl.multiple_of(tb * BUILD_GROUP + sub * SUBLANES, SUBLANES)
                r1 = e1_ref[pl.ds(base, SUBLANES), :]
                r2 = e2_ref[pl.ds(base, SUBLANES), :]
                gg = gate_ref[pl.ds(base, SUBLANES), :]
                for i in range(SUBLANES):
                    a = jnp.where(key_id == r1[i:i + 1, :], gg[i:i + 1, :], 0.0).astype(BF16)
                    b = jnp.where(key_id == r2[i:i + 1, :], 1.0, 0.0).astype(BF16)
                    gm = lax.dot_general(a, b, (((1,), (1,)), ((), ())),
                                         preferred_element_type=F32)
                    bits = lax.bitcast_convert_type(gm, I32) + 32768
                    packed = ((bits[:GATE_ROWS] & HI_MASK)
                              | lax.shift_right_logical(bits[GATE_ROWS:], 16))
                    gmat_scr[pl.ds((base + i) * GATE_PITCH, GATE_ROWS), :] = packed
            return carry

        lax.fori_loop(0, tp // BUILD_GROUP, body, 0)

    h2 = h2_ref[...]
    for jb in range(ec // MXU_DEPTH):
        lo = jb * MXU_DEPTH
        s = jnp.dot(h2, ut_ref[0, :, lo:lo + MXU_DEPTH], preferred_element_type=F32)
        for u in range(MXU_DEPTH // LANES):
            k1 = c * nsub + jb * (MXU_DEPTH // LANES) + u
            word = gmat_scr[pl.ds(k1 % GATE_ROWS, tp, stride=GATE_PITCH), :]
            shift = (k1 // GATE_ROWS) * 16
            gi = lax.bitcast_convert_type(lax.shift_left(word, shift) & HI_MASK, F32)
            su = s[:, u * LANES:(u + 1) * LANES]
            w_scr[:, lo + u * LANES:lo + (u + 1) * LANES] = (
                (0.5 * su) * (1.0 + lax.erf(su * INV_SQRT2)) * gi).astype(BF16)
    acc_scr[...] += jnp.dot(w_scr[...], v_ref[...], preferred_element_type=F32)

    @pl.when(c == pl.num_programs(1) - 1)
    def _finish():
        xo = x1_ref[...] + gt2_ref[0] * acc_scr[...]
        y_ref[...] = _rms(xo) * gf_ref[...]


def _peer(h2_flat, e1, e2, gate, ut_bf, v_bf, x1_flat, gt2, g_final, tokens_per_batch, tp):
    t, d = h2_flat.shape
    ec = ut_bf.shape[2]
    ne = v_bf.shape[0]
    nslot = e1.shape[1]
    tiles_per_batch = tokens_per_batch // tp
    tok = lambda i, c: (i, 0)
    once = pl.Buffered(1)
    return pl.pallas_call(
        functools.partial(_peer_kernel, tp=tp, ec=ec),
        grid=(t // tp, ne // ec),
        in_specs=[pl.BlockSpec((tp, d), tok),
                  pl.BlockSpec((tp, nslot), tok),
                  pl.BlockSpec((tp, nslot), tok),
                  pl.BlockSpec((tp, nslot), tok),
                  pl.BlockSpec((1, d, ec), lambda i, c: (c, 0, 0)),
                  pl.BlockSpec((ec, d), lambda i, c: (c, 0)),
                  pl.BlockSpec((tp, d), tok, pipeline_mode=once),
                  pl.BlockSpec((1, 1, d), lambda i, c: (i // tiles_per_batch, 0, 0)),
                  pl.BlockSpec((1, d), lambda i, c: (0, 0))],
        out_specs=pl.BlockSpec((tp, d), tok),
        out_shape=jax.ShapeDtypeStruct((t, d), F32),
        scratch_shapes=[pltpu.VMEM((tp * GATE_PITCH, LANES), I32),
                        pltpu.VMEM((tp, ec), BF16),
                        pltpu.VMEM((tp, d), F32)],
        compiler_params=_cparams(("parallel", "arbitrary")),
        name="peer",
    )(h2_flat, e1, e2, gate, ut_bf, v_bf, x1_flat, gt2, g_final)


def _rope_tables(s):
    inv_freq = 1.0 / (ROPE_THETA ** (jnp.arange(0, HEAD_DIM, 2, dtype=F32) / HEAD_DIM))
    ang = jnp.arange(s, dtype=F32)[:, None] * inv_freq[None, :]
    ang = jnp.concatenate([ang, ang, ang, ang], axis=-1)
    cos, sin = jnp.cos(ang), jnp.sin(ang)
    low = (jnp.arange(LANES) % HEAD_DIM) < (HEAD_DIM // 2)
    return cos, jnp.where(low, -sin, 0.0), jnp.where(low, 0.0, sin)


def _pick_tile(n, want):
    t = min(n, want)
    while n % t:
        t //= 2
    return t


def _trunk(x, mod, wts, lam_init):
    b, s, d = x.shape
    sh1, sc1, gt1, sh2, sc2, gt2 = [m.reshape(b, 1, d) for m in jnp.split(mod, 6, axis=-1)]
    cos, sa, sb = _rope_tables(s)
    p, q, k, v = _front(x, sc1, sh1, wts["g_norm1"], wts["w_in"], cos, sa, sb, _pick_tile(s, 512))
    mixed = _pool(p, wts["pool_w"], wts["pool_scale"])
    att = _attention(q, k, v, wts["lamv"], wts["g_sub"], lam_init, _pick_tile(s, 512),
                     _pick_tile(s, 512))
    x1, h2 = _merge(x, mixed, att, (sc1, sh1, gt1, sc2, sh2), wts["g_norm1"], wts["g_norm2"],
                    wts["w_pool_out"], wts["w_att_out"], wts["w_gate"], wts["b_gate"],
                    wts["w_out"], _pick_tile(s, 512))
    t = b * s
    h2f = h2.reshape(t, d)
    e1, e2, gate = _route(h2f, wts["w_peer_qt"], wts["peer_keys"])
    y = _peer(h2f, e1, e2, gate, wts["peer_ut"], wts["peer_v"], x1.reshape(t, d),
              gt2, wts["g_final"], s, _pick_tile(s, 512))
    return y.reshape(b, s, d)


def kernel(x_prompt, x_sample, c_prompt, c_sample, w_ada, b_ada, g_norm1, w_in, pool_w, pool_scale,
           w_pool_out, lam_q1, lam_k1, lam_q2, lam_k2, g_sub, w_att_out, w_gate, b_gate, w_out,
           g_norm2, w_peer_q, peer_keys, peer_u, peer_v, g_final):
    depth = w_ada.shape[0]
    assert depth == 1, "single-layer trunk"
    l = 0
    lam_init = 0.8 - 0.6 * math.exp(-0.3 * l)
    d = x_prompt.shape[-1]
    wts = {
        "g_norm1": g_norm1[l][None, :], "g_norm2": g_norm2[l][None, :], "g_final": g_final[None, :],
        "w_in": w_in[l].astype(BF16),
        "pool_w": pool_w[l].astype(BF16), "pool_scale": pool_scale[l][None, :],
        "w_pool_out": w_pool_out[l].astype(BF16),
        "lamv": jnp.stack([lam_q1[l], lam_k1[l], lam_q2[l], lam_k2[l]]).astype(F32),
        "g_sub": g_sub[l][None, :],
        "w_att_out": w_att_out[l].astype(BF16),
        "w_gate": w_gate[l].astype(BF16), "b_gate": b_gate[l][None, :],
        "w_out": w_out[l].astype(BF16),
        "w_peer_qt": w_peer_q[l].T.astype(BF16),
        "peer_keys": peer_keys[l].astype(BF16),
        "peer_ut": peer_u[l].astype(BF16).reshape(-1, PEER_CHUNK, d).transpose(0, 2, 1),
        "peer_v": peer_v[l].astype(BF16),
    }
    nb = x_prompt.shape[0]
    mod = _ada(jnp.concatenate([c_prompt, c_sample], axis=0), w_ada[l].astype(BF16), b_ada[l][None, :])
    y_prompt = _trunk(x_prompt, mod[:nb], wts, lam_init)
    y_sample = _trunk(x_sample, mod[nb:], wts, lam_init)
    return (y_prompt, y_sample)
```

```python
import functools
import math

import jax
import jax.numpy as jnp
from jax import lax
from jax.experimental import pallas as pl
from jax.experimental.pallas import tpu as pltpu

F32 = jnp.float32
BF16 = jnp.bfloat16
I32 = jnp.int32

NORM_EPS = 1e-6
ROPE_THETA = 10000.0
POOL_WINDOWS = (2, 4, 8, 16)
POOL_PAD = 16
N_HEADS = 8
HEAD_DIM = 64
LANES = 128
MXU_DEPTH = 256
TOPK = 16
CAND_ROWS = tuple((a, TOPK // (a + 1)) for a in range(TOPK))
VMEM_LIMIT = 56 * 1024 * 1024
INV_SQRT2 = 1.0 / math.sqrt(2.0)
LOG2E = 1.0 / math.log(2.0)
NEG_INF = float("-inf")


def _cparams(sem):
    return pltpu.CompilerParams(dimension_semantics=sem, vmem_limit_bytes=VMEM_LIMIT)


def _rms(x):
    return x * lax.rsqrt(jnp.mean(x * x, axis=-1, keepdims=True) + NORM_EPS)


def _ada_kernel(c_ref, w_ref, b_ref, o_ref):
    c = c_ref[...]
    a = (c * jax.nn.sigmoid(c)).astype(BF16)
    o_ref[...] = jnp.dot(a, w_ref[...], preferred_element_type=F32) + b_ref[...]


def _ada(c, w_bf, b):
    n, d = c.shape
    nout = w_bf.shape[1]
    return pl.pallas_call(
        _ada_kernel,
        grid=(nout // d,),
        in_specs=[pl.BlockSpec((n, d), lambda j: (0, 0)),
                  pl.BlockSpec((d, d), lambda j: (0, j)),
                  pl.BlockSpec((1, d), lambda j: (0, j))],
        out_specs=pl.BlockSpec((n, d), lambda j: (0, j)),
        out_shape=jax.ShapeDtypeStruct((n, nout), F32),
        compiler_params=_cparams(("parallel",)),
        name="ada",
    )(c, w_bf, b)


def _front_kernel(x_ref, sc_ref, sh_ref, g_ref, w_ref, cos_ref, sa_ref, sb_ref,
                  p_ref, q_ref, k_ref, v_ref, *, d, pool_dim):
    x = x_ref[0]
    h = (_rms(x) * g_ref[...]) * (1.0 + sc_ref[0]) + sh_ref[0]
    proj = jnp.dot(h.astype(BF16), w_ref[...], preferred_element_type=F32)
    p_ref[0] = proj[:, :pool_dim]
    cos, sa, sb = cos_ref[...], sa_ref[...], sb_ref[...]
    half = HEAD_DIM // 2

    def rope(t):
        return (t * cos + pltpu.roll(t, LANES - half, axis=1) * sa
                + pltpu.roll(t, half, axis=1) * sb)

    qscale = HEAD_DIM ** -0.5 * LOG2E
    for j in range(d // LANES):
        o = pool_dim + j * LANES
        q_ref[0, j] = (rope(proj[:, o:o + LANES]) * qscale).astype(BF16)
        o = pool_dim + d + j * LANES
        k_ref[0, j] = rope(proj[:, o:o + LANES]).astype(BF16)
        o = pool_dim + 2 * d + j * LANES
        v_ref[0, j] = proj[:, o:o + LANES].astype(BF16)


def _front(x, sc, sh, g, w_in_bf, cos, sa, sb, tm):
    b, s, d = x.shape
    in_dim = w_in_bf.shape[1]
    pool_dim = in_dim - 3 * d
    nh = d // LANES
    row = lambda i, j: (i, j, 0)
    per_b = lambda i, j: (i, 0, 0)
    const = lambda i, j: (0, 0)
    return pl.pallas_call(
        functools.partial(_front_kernel, d=d, pool_dim=pool_dim),
        grid=(b, s // tm),
        in_specs=[pl.BlockSpec((1, tm, d), row),
                  pl.BlockSpec((1, 1, d), per_b),
                  pl.BlockSpec((1, 1, d), per_b),
                  pl.BlockSpec((1, d), const),
                  pl.BlockSpec((d, in_dim), const),
                  pl.BlockSpec((tm, LANES), lambda i, j: (j, 0)),
                  pl.BlockSpec((tm, LANES), lambda i, j: (j, 0)),
                  pl.BlockSpec((tm, LANES), lambda i, j: (j, 0))],
        out_specs=[pl.BlockSpec((1, tm, pool_dim), row)]
                  + [pl.BlockSpec((1, nh, tm, LANES), lambda i, j: (i, 0, j, 0))] * 3,
        out_shape=[jax.ShapeDtypeStruct((b, s, pool_dim), F32)]
                  + [jax.ShapeDtypeStruct((b, nh, s, LANES), BF16)] * 3,
        compiler_params=_cparams(("parallel", "parallel")),
        name="front",
    )(x, sc, sh, g, w_in_bf, cos, sa, sb)


def _pool_kernel(p_ref, w_ref, scale_ref, o_ref, pad_ref, *, s):
    pos = lax.broadcasted_iota(I32, (s, LANES), 0)
    zeros = jnp.zeros((POOL_PAD, LANES), F32)
    for g, win in enumerate(POOL_WINDOWS):
        half = win // 2
        xg = p_ref[0, :, g * LANES:(g + 1) * LANES]
        pad_ref[0:POOL_PAD, :] = zeros
        pad_ref[POOL_PAD:POOL_PAD + s, :] = xg
        pad_ref[POOL_PAD + s:2 * POOL_PAD + s, :] = zeros
        wsum = pad_ref[POOL_PAD - half:POOL_PAD - half + s, :]
        for dlt in range(-half + 1, half):
            wsum = wsum + pad_ref[POOL_PAD + dlt:POOL_PAD + dlt + s, :]
        cnt = (jnp.minimum(pos + half, s) - jnp.maximum(pos - half, 0)).astype(F32)
        diff = wsum / cnt - xg
        mixed = jnp.dot(diff.astype(BF16), w_ref[g], preferred_element_type=F32)
        o_ref[0, :, g * LANES:(g + 1) * LANES] = (
            mixed * scale_ref[:, g * LANES:(g + 1) * LANES]).astype(BF16)


def _pool(p, pool_w_bf, pool_scale):
    b, s, pool_dim = p.shape
    ng = pool_w_bf.shape[0]
    return pl.pallas_call(
        functools.partial(_pool_kernel, s=s),
        grid=(b,),
        in_specs=[pl.BlockSpec((1, s, pool_dim), lambda i: (i, 0, 0)),
                  pl.BlockSpec((ng, LANES, LANES), lambda i: (0, 0, 0)),
                  pl.BlockSpec((1, pool_dim), lambda i: (0, 0))],
        out_specs=pl.BlockSpec((1, s, pool_dim), lambda i: (i, 0, 0)),
        out_shape=jax.ShapeDtypeStruct((b, s, pool_dim), BF16),
        scratch_shapes=[pltpu.VMEM((s + 2 * POOL_PAD, LANES), F32)],
        compiler_params=_cparams(("parallel",)),
        name="pool",
    )(p, pool_w_bf, pool_scale)


def _attn_kernel(lamv_ref, q_ref, k_ref, v_ref, gsub_ref, o_ref, s_scr, vext_scr,
                 *, lam_init, s, tq, tk):
    lv = lamv_ref[...]
    lam = (jnp.exp(jnp.sum(lv[0:1] * lv[1:2], axis=-1, keepdims=True))
           - jnp.exp(jnp.sum(lv[2:3] * lv[3:4], axis=-1, keepdims=True)) + lam_init)
    @pl.when(pl.program_id(2) == 0)
    def _extend_values():
        vext_scr[:, :LANES] = v_ref[0, 0]
        vext_scr[:, LANES:] = jnp.ones((s, LANES), BF16)

    q = q_ref[0, 0]
    first = lax.broadcasted_iota(I32, (1, LANES), 1) < HEAD_DIM
    zero = jnp.zeros_like(q)
    qs = (jnp.where(first, q, zero), jnp.where(first, zero, q))
    nt = (((1,), (1,)), ((), ()))
    nchunk = s // tk
    mx = [jnp.full((tq, LANES), NEG_INF, F32), jnp.full((tq, LANES), NEG_INF, F32)]
    for c in range(nchunk):
        kc = k_ref[0, 0, c * tk:(c + 1) * tk, :]
        for u in range(2):
            sc = lax.dot_general(qs[u], kc, nt, preferred_element_type=F32)
            s_scr[u, :, c * tk:(c + 1) * tk] = sc
            for jj in range(tk // LANES):
                mx[u] = jnp.maximum(mx[u], sc[:, jj * LANES:(jj + 1) * LANES])
    mx = [jnp.max(m, axis=-1, keepdims=True) for m in mx]
    acc = [jnp.zeros((tq, 2 * LANES), F32), jnp.zeros((tq, 2 * LANES), F32)]
    for c in range(nchunk):
        vc = vext_scr[c * tk:(c + 1) * tk, :]
        for u in range(2):
            p = jnp.exp2(s_scr[u, :, c * tk:(c + 1) * tk] - mx[u]).astype(BF16)
            acc[u] = acc[u] + jnp.dot(p, vc, preferred_element_type=F32)
    c1 = 1.0 / acc[0][:, LANES:LANES + 1]
    c2 = lam / acc[1][:, LANES:LANES + 1]
    o = acc[0][:, :LANES] * c1 - acc[1][:, :LANES] * c2
    o_ref[0, 0] = ((_rms(o) * gsub_ref[...]) * (1.0 - lam_init)).astype(BF16)


def _attention(q, k, v, lamv, g_sub, lam_init, tq, tk):
    b, nh, s, _ = q.shape
    return pl.pallas_call(
        functools.partial(_attn_kernel, lam_init=lam_init, s=s, tq=tq, tk=tk),
        grid=(b, nh, s // tq),
        in_specs=[pl.BlockSpec((4, HEAD_DIM), lambda i, h, j: (0, 0)),
                  pl.BlockSpec((1, 1, tq, LANES), lambda i, h, j: (i, h, j, 0)),
                  pl.BlockSpec((1, 1, s, LANES), lambda i, h, j: (i, h, 0, 0)),
                  pl.BlockSpec((1, 1, s, LANES), lambda i, h, j: (i, h, 0, 0)),
                  pl.BlockSpec((1, LANES), lambda i, h, j: (0, 0))],
        out_specs=pl.BlockSpec((1, 1, tq, LANES), lambda i, h, j: (i, h, j, 0)),
        out_shape=jax.ShapeDtypeStruct((b, nh, s, LANES), BF16),
        scratch_shapes=[pltpu.VMEM((2, tq, s), F32), pltpu.VMEM((s, 2 * LANES), BF16)],
        compiler_params=_cparams(("parallel", "parallel", "arbitrary")),
        name="attn",
    )(lamv, q, k, v, g_sub)


def _merge_kernel(x_ref, mixed_ref, att_ref, sc1_ref, sh1_ref, gt1_ref, sc2_ref, sh2_ref,
                  g1_ref, g2_ref, wpool_ref, watt_ref, wgate_ref, bgate_ref, wout_ref,
                  x1_ref, h2_ref, *, d):
    x = x_ref[0]
    h = ((_rms(x) * g1_ref[...]) * (1.0 + sc1_ref[0]) + sh1_ref[0]).astype(BF16)
    gates = jax.nn.sigmoid(jnp.dot(h, wgate_ref[...], preferred_element_type=F32) + bgate_ref[...])
    y_pool = jnp.dot(mixed_ref[0], wpool_ref[...], preferred_element_type=F32)
    att = jnp.concatenate([att_ref[0, hd] for hd in range(d // LANES)], axis=1)
    y_att = jnp.dot(att, watt_ref[...], preferred_element_type=F32)
    merged = gates[:, :d] * y_pool + gates[:, d:] * y_att
    x1 = x + gt1_ref[0] * jnp.dot(merged.astype(BF16), wout_ref[...], preferred_element_type=F32)
    x1_ref[0] = x1
    h2 = (_rms(x1) * g2_ref[...]) * (1.0 + sc2_ref[0]) + sh2_ref[0]
    h2_ref[0] = h2.astype(BF16)


def _merge(x, mixed, att, mods, g1, g2, wpool, watt, wgate, bgate, wout, tm):
    b, s, d = x.shape
    pool_dim = mixed.shape[-1]
    row = lambda i, j: (i, j, 0)
    per_b = lambda i, j: (i, 0, 0)
    const = lambda i, j: (0, 0)
    sc1, sh1, gt1, sc2, sh2 = mods
    return pl.pallas_call(
        functools.partial(_merge_kernel, d=d),
        grid=(b, s // tm),
        in_specs=[pl.BlockSpec((1, tm, d), row),
                  pl.BlockSpec((1, tm, pool_dim), row),
                  pl.BlockSpec((1, d // LANES, tm, LANES), lambda i, j: (i, 0, j, 0))]
                 + [pl.BlockSpec((1, 1, d), per_b)] * 5
                 + [pl.BlockSpec((1, d), const), pl.BlockSpec((1, d), const),
                    pl.BlockSpec((pool_dim, d), const), pl.BlockSpec((d, d), const),
                    pl.BlockSpec((d, 2 * d), const), pl.BlockSpec((1, 2 * d), const),
                    pl.BlockSpec((d, d), const)],
        out_specs=[pl.BlockSpec((1, tm, d), row), pl.BlockSpec((1, tm, d), row)],
        out_shape=[jax.ShapeDtypeStruct((b, s, d), F32), jax.ShapeDtypeStruct((b, s, d), BF16)],
        compiler_params=_cparams(("parallel", "parallel")),
        name="merge",
    )(x, mixed, att, sc1, sh1, gt1, sc2, sh2, g1, g2, wpool, watt, wgate, bgate, wout)


ROUTE_TOKENS = 8 * LANES
LAY_PITCH = LANES + 8


def _sort16_network():
    n, pairs, p = TOPK, [], 1
    while p < n:
        k = p
        while k >= 1:
            for j in range(k % p, n - k, 2 * k):
                for i in range(min(k, n - j - k)):
                    if (i + j) // (2 * p) == (i + j + k) // (2 * p):
                        pairs.append((i + j, i + j + k))
            k //= 2
        p *= 2
    return tuple(pairs)


SORT16 = _sort16_network()


def _swap_mask(a, b):
    (av, ai), (bv, bi) = a, b
    return (bv > av) | ((bv == av) & (bi < ai))


def _order(a, b):
    sw = _swap_mask(a, b)
    first = (jnp.maximum(a[0], b[0]), jnp.where(sw, b[1], a[1]))
    second = (jnp.minimum(a[0], b[0]), jnp.where(sw, a[1], b[1]))
    return first, second


def _first_of(a, b):
    return jnp.maximum(a[0], b[0]), jnp.where(_swap_mask(a, b), b[1], a[1])


def _sort16(items):
    items = list(items)
    for i, j in SORT16:
        items[i], items[j] = _order(items[i], items[j])
    return items


def _merge_top16(a, b):
    c = [_first_of(a[i], b[TOPK - 1 - i]) for i in range(TOPK)]
    stride = TOPK // 2
    while stride:
        for i in range(TOPK):
            if not i & stride:
                c[i], c[i + stride] = _order(c[i], c[i + stride])
        stride //= 2
    return c


def _route_kernel(h2_ref, wqt_ref, keys_ref, e1_ref, e2_ref, gate_ref,
                  qt_scr, lay_scr, tv_scr, ti_scr):
    nblk = ROUTE_TOKENS // LANES
    qt = lax.dot_general(wqt_ref[...], h2_ref[...], (((1,), (1,)), ((), ())),
                         preferred_element_type=F32)
    qt_scr[...] = qt.astype(BF16)

    def set_body(si, carry):
        r0 = pl.multiple_of(si * HEAD_DIM, HEAD_DIM)
        st = jnp.dot(keys_ref[si], qt_scr[pl.ds(r0, HEAD_DIM), :], preferred_element_type=F32)
        for c in range(nblk):
            lay_scr[c * LAY_PITCH:c * LAY_PITCH + LANES, :] = st[:, c * LANES:(c + 1) * LANES]
        lists = []
        for g in range(LANES // TOPK):
            items = []
            for k in range(g * TOPK, (g + 1) * TOPK):
                items.append((lay_scr[pl.ds(k, nblk, stride=LAY_PITCH), :],
                              jnp.full((nblk, LANES), k, I32)))
            lists.append(_sort16(items))
        while len(lists) > 1:
            lists = [_merge_top16(lists[2 * i], lists[2 * i + 1]) for i in range(len(lists) // 2)]
        for a, (v, idx) in enumerate(lists[0]):
            tv_scr[si * TOPK + a] = v
            ti_scr[si * TOPK + a] = idx
        return carry

    lax.fori_loop(0, 2 * N_HEADS, set_body, 0)

    def head_body(h, carry):
        b1 = 2 * h * TOPK
        b2 = b1 + TOPK
        cv, ce = [], []
        for a, nb in CAND_ROWS:
            va = tv_scr[b1 + a]
            ea = ti_scr[b1 + a] * LANES
            for b in range(nb):
                cv.append(va + tv_scr[b2 + b])
                ce.append(ea + ti_scr[b2 + b])
        best_s, best_e = [], []
        for _ in range(TOPK):
            nodes = list(zip(cv, ce))
            while len(nodes) > 1:
                nxt = []
                for i in range(0, len(nodes) - 1, 2):
                    (lv, le), (rv, re) = nodes[i], nodes[i + 1]
                    nxt.append((jnp.maximum(lv, rv), jnp.where(rv > lv, re, le)))
                if len(nodes) % 2:
                    nxt.append(nodes[-1])
                nodes = nxt
            m, e = nodes[0]
            best_s.append(m)
            best_e.append(e)
            cv = [jnp.where(c_e == e, NEG_INF, c_v) for c_v, c_e in zip(cv, ce)]
        ex = [jnp.exp(b - best_s[0]) for b in best_s]
        z = ex[0]
        for t in ex[1:]:
            z = z + t
        for k in range(TOPK):
            e1_ref[h * TOPK + k] = lax.shift_right_logical(best_e[k], 7)
            e2_ref[h * TOPK + k] = best_e[k] & (LANES - 1)
            gate_ref[h * TOPK + k] = ex[k] / z
        return carry

    lax.fori_loop(0, N_HEADS, head_body, 0)


def _route(h2_flat, wqt_bf, keys_bf):
    t, d = h2_flat.shape
    assert t % ROUTE_TOKENS == 0
    nslot = N_HEADS * TOPK
    nblk = ROUTE_TOKENS // LANES
    keys2 = keys_bf.reshape(2 * N_HEADS, LANES, HEAD_DIM)
    out = pl.BlockSpec((nslot, nblk, LANES), lambda i: (0, i, 0))
    outs = pl.pallas_call(
        _route_kernel,
        grid=(t // ROUTE_TOKENS,),
        in_specs=[pl.BlockSpec((ROUTE_TOKENS, d), lambda i: (i, 0)),
                  pl.BlockSpec(wqt_bf.shape, lambda i: (0, 0)),
                  pl.BlockSpec(keys2.shape, lambda i: (0, 0, 0))],
        out_specs=[out, out, out],
        out_shape=[jax.ShapeDtypeStruct((nslot, t // LANES, LANES), I32),
                   jax.ShapeDtypeStruct((nslot, t // LANES, LANES), I32),
                   jax.ShapeDtypeStruct((nslot, t // LANES, LANES), F32)],
        scratch_shapes=[pltpu.VMEM((wqt_bf.shape[0], ROUTE_TOKENS), BF16),
                        pltpu.VMEM((nblk * LAY_PITCH, LANES), F32),
                        pltpu.VMEM((2 * N_HEADS * TOPK, nblk, LANES), F32),
                        pltpu.VMEM((2 * N_HEADS * TOPK, nblk, LANES), I32)],
        compiler_params=_cparams(("parallel",)),
        name="route",
    )(h2_flat, wqt_bf, keys2)
    return [o.reshape(nslot, t).T for o in outs]


PHASE_KEYS = LANES // 2
GATE_ROWS = PHASE_KEYS // 2
GATE_PITCH = GATE_ROWS + 4
SUBLANES = 8
BUILD_GROUP = 64
HI_MASK = -65536
PEER_CHUNK = 1024
PEER_TOKENS = 1024


def _peer_kernel(h2_ref, e1_ref, e2_ref, gate_ref, ut_ref, v_ref, x1_ref, gt2_ref, gf_ref,
                 y_ref, gmat_scr, w_scr, *, tp, ec):
    c = pl.program_id(1)
    nsub = ec // LANES
    steps_per_phase = PHASE_KEYS // nsub

    @pl.when(c == 0)
    def _zero_accumulator():
        y_ref[...] = jnp.zeros_like(y_ref)

    @pl.when(c % steps_per_phase == 0)
    def _build_gate_matrices():
        first_key = (c // steps_per_phase) * PHASE_KEYS
        key_a = lax.broadcasted_iota(I32, (PHASE_KEYS, LANES), 0) + first_key
        key_b = lax.broadcasted_iota(I32, (LANES, LANES), 0)

        def body(tb, carry):
            for sub in range(BUILD_GROUP // SUBLANES):
                base = pl.multiple_of(tb * BUILD_GROUP + sub * SUBLANES, SUBLANES)
                r1 = e1_ref[pl.ds(base, SUBLANES), :]
                r2 = e2_ref[pl.ds(base, SUBLANES), :]
                gg = gate_ref[pl.ds(base, SUBLANES), :]
                for i in range(SUBLANES):
                    a = jnp.where(key_a == r1[i:i + 1, :], gg[i:i + 1, :], 0.0).astype(BF16)
                    b = jnp.where(key_b == r2[i:i + 1, :], 1.0, 0.0).astype(BF16)
                    gm = lax.dot_general(a, b, (((1,), (1,)), ((), ())),
                                         preferred_element_type=F32)
                    bits = lax.bitcast_convert_type(gm, I32) + 32768
                    packed = ((bits[:GATE_ROWS] & HI_MASK)
                              | lax.shift_right_logical(bits[GATE_ROWS:], 16))
                    gmat_scr[pl.ds((base + i) * GATE_PITCH, GATE_ROWS), :] = packed
            return carry

        lax.fori_loop(0, tp // BUILD_GROUP, body, 0)

    h2 = h2_ref[...]
    for jb in range(ec // MXU_DEPTH):
        lo = jb * MXU_DEPTH
        s = jnp.dot(h2, ut_ref[0, :, lo:lo + MXU_DEPTH], preferred_element_type=F32)
        for u in range(MXU_DEPTH // LANES):
            k1 = (c * nsub + jb * (MXU_DEPTH // LANES) + u) % PHASE_KEYS
            word = gmat_scr[pl.ds(k1 % GATE_ROWS, tp, stride=GATE_PITCH), :]
            shift = (k1 // GATE_ROWS) * 16
            gi = lax.bitcast_convert_type(lax.shift_left(word, shift) & HI_MASK, F32)
            su = s[:, u * LANES:(u + 1) * LANES]
            w_scr[:, lo + u * LANES:lo + (u + 1) * LANES] = (
                (0.5 * su) * (1.0 + lax.erf(su * INV_SQRT2)) * gi).astype(BF16)
    y_ref[...] += jnp.dot(w_scr[...], v_ref[...], preferred_element_type=F32)

    @pl.when(c == pl.num_programs(1) - 1)
    def _finish():
        xo = x1_ref[...] + gt2_ref[0] * y_ref[...]
        y_ref[...] = _rms(xo) * gf_ref[...]


def _peer(h2_flat, e1, e2, gate, ut_bf, v_bf, x1_flat, gt2, g_final, tokens_per_batch, tp):
    t, d = h2_flat.shape
    ec = ut_bf.shape[2]
    ne = v_bf.shape[0]
    nslot = e1.shape[1]
    tiles_per_batch = tokens_per_batch // tp
    tok = lambda i, c: (i, 0)
    once = pl.Buffered(1)
    return pl.pallas_call(
        functools.partial(_peer_kernel, tp=tp, ec=ec),
        grid=(t // tp, ne // ec),
        in_specs=[pl.BlockSpec((tp, d), tok),
                  pl.BlockSpec((tp, nslot), tok),
                  pl.BlockSpec((tp, nslot), tok),
                  pl.BlockSpec((tp, nslot), tok),
                  pl.BlockSpec((1, d, ec), lambda i, c: (c, 0, 0)),
                  pl.BlockSpec((ec, d), lambda i, c: (c, 0)),
                  pl.BlockSpec((tp, d), tok, pipeline_mode=once),
                  pl.BlockSpec((1, 1, d), lambda i, c: (i // tiles_per_batch, 0, 0)),
                  pl.BlockSpec((1, d), lambda i, c: (0, 0))],
        out_specs=pl.BlockSpec((tp, d), tok),
        out_shape=jax.ShapeDtypeStruct((t, d), F32),
        scratch_shapes=[pltpu.VMEM((tp * GATE_PITCH, LANES), I32),
                        pltpu.VMEM((tp, ec), BF16)],
        compiler_params=_cparams(("parallel", "arbitrary")),
        name="peer",
    )(h2_flat, e1, e2, gate, ut_bf, v_bf, x1_flat, gt2, g_final)


def _rope_tables(s):
    inv_freq = 1.0 / (ROPE_THETA ** (jnp.arange(0, HEAD_DIM, 2, dtype=F32) / HEAD_DIM))
    ang = jnp.arange(s, dtype=F32)[:, None] * inv_freq[None, :]
    ang = jnp.concatenate([ang, ang, ang, ang], axis=-1)
    cos, sin = jnp.cos(ang), jnp.sin(ang)
    low = (jnp.arange(LANES) % HEAD_DIM) < (HEAD_DIM // 2)
    return cos, jnp.where(low, -sin, 0.0), jnp.where(low, 0.0, sin)


def _pick_tile(n, want):
    t = min(n, want)
    while n % t:
        t //= 2
    return t


def _trunk(x, mod, wts, lam_init):
    b, s, d = x.shape
    sh1, sc1, gt1, sh2, sc2, gt2 = [m.reshape(b, 1, d) for m in jnp.split(mod, 6, axis=-1)]
    cos, sa, sb = _rope_tables(s)
    p, q, k, v = _front(x, sc1, sh1, wts["g_norm1"], wts["w_in"], cos, sa, sb, _pick_tile(s, 512))
    mixed = _pool(p, wts["pool_w"], wts["pool_scale"])
    att = _attention(q, k, v, wts["lamv"], wts["g_sub"], lam_init, _pick_tile(s, 512),
                     _pick_tile(s, 512))
    x1, h2 = _merge(x, mixed, att, (sc1, sh1, gt1, sc2, sh2), wts["g_norm1"], wts["g_norm2"],
                    wts["w_pool_out"], wts["w_att_out"], wts["w_gate"], wts["b_gate"],
                    wts["w_out"], _pick_tile(s, 512))
    t = b * s
    h2f = h2.reshape(t, d)
    e1, e2, gate = _route(h2f, wts["w_peer_qt"], wts["peer_keys"])
    y = _peer(h2f, e1, e2, gate, wts["peer_ut"], wts["peer_v"], x1.reshape(t, d),
              gt2, wts["g_final"], s, _pick_tile(s, PEER_TOKENS))
    return y.reshape(b, s, d)


def kernel(x_prompt, x_sample, c_prompt, c_sample, w_ada, b_ada, g_norm1, w_in, pool_w, pool_scale,
           w_pool_out, lam_q1, lam_k1, lam_q2, lam_k2, g_sub, w_att_out, w_gate, b_gate, w_out,
           g_norm2, w_peer_q, peer_keys, peer_u, peer_v, g_final):
    depth = w_ada.shape[0]
    assert depth == 1, "single-layer trunk"
    l = 0
    lam_init = 0.8 - 0.6 * math.exp(-0.3 * l)
    d = x_prompt.shape[-1]
    wts = {
        "g_norm1": g_norm1[l][None, :], "g_norm2": g_norm2[l][None, :], "g_final": g_final[None, :],
        "w_in": w_in[l].astype(BF16),
        "pool_w": pool_w[l].astype(BF16), "pool_scale": pool_scale[l][None, :],
        "w_pool_out": w_pool_out[l].astype(BF16),
        "lamv": jnp.stack([lam_q1[l], lam_k1[l], lam_q2[l], lam_k2[l]]).astype(F32),
        "g_sub": g_sub[l][None, :],
        "w_att_out": w_att_out[l].astype(BF16),
        "w_gate": w_gate[l].astype(BF16), "b_gate": b_gate[l][None, :],
        "w_out": w_out[l].astype(BF16),
        "w_peer_qt": w_peer_q[l].T.astype(BF16),
        "peer_keys": peer_keys[l].astype(BF16),
        "peer_ut": peer_u[l].astype(BF16).reshape(-1, PEER_CHUNK, d).transpose(0, 2, 1),
        "peer_v": peer_v[l].astype(BF16),
    }
    nb = x_prompt.shape[0]
    mod = _ada(jnp.concatenate([c_prompt, c_sample], axis=0), w_ada[l].astype(BF16), b_ada[l][None, :])
    y_prompt = _trunk(x_prompt, mod[:nb], wts, lam_init)
    y_sample = _trunk(x_sample, mod[nb:], wts, lam_init)
    return (y_prompt, y_sample)
```

```python
import functools
import math

import jax
import jax.numpy as jnp
from jax import lax
from jax.experimental import pallas as pl
from jax.experimental.pallas import tpu as pltpu

F32 = jnp.float32
BF16 = jnp.bfloat16
I32 = jnp.int32

NORM_EPS = 1e-6
ROPE_THETA = 10000.0
POOL_WINDOWS = (2, 4, 8, 16)
POOL_PAD = 16
N_HEADS = 8
HEAD_DIM = 64
LANES = 128
MXU_DEPTH = 256
TOPK = 16
CAND_ROWS = tuple((a, TOPK // (a + 1)) for a in range(TOPK))
VMEM_LIMIT = 58 * 1024 * 1024
INV_SQRT2 = 1.0 / math.sqrt(2.0)
LOG2E = 1.0 / math.log(2.0)
NEG_INF = float("-inf")


def _cparams(sem):
    return pltpu.CompilerParams(dimension_semantics=sem, vmem_limit_bytes=VMEM_LIMIT)


def _rms(x):
    return x * lax.rsqrt(jnp.mean(x * x, axis=-1, keepdims=True) + NORM_EPS)


def _ada_kernel(c_ref, w_ref, b_ref, o_ref):
    c = c_ref[...]
    a = (c * jax.nn.sigmoid(c)).astype(BF16)
    o_ref[...] = jnp.dot(a, w_ref[...], preferred_element_type=F32) + b_ref[...]


def _ada(c, w_bf, b):
    n, d = c.shape
    nout = w_bf.shape[1]
    return pl.pallas_call(
        _ada_kernel,
        grid=(nout // d,),
        in_specs=[pl.BlockSpec((n, d), lambda j: (0, 0)),
                  pl.BlockSpec((d, d), lambda j: (0, j)),
                  pl.BlockSpec((1, d), lambda j: (0, j))],
        out_specs=pl.BlockSpec((n, d), lambda j: (0, j)),
        out_shape=jax.ShapeDtypeStruct((n, nout), F32),
        compiler_params=_cparams(("parallel",)),
        name="ada",
    )(c, w_bf, b)


def _front_kernel(x_ref, sc_ref, sh_ref, g_ref, w_ref, cos_ref, sa_ref, sb_ref,
                  p_ref, q_ref, k_ref, v_ref, *, d, pool_dim):
    x = x_ref[0]
    h = (_rms(x) * g_ref[...]) * (1.0 + sc_ref[0]) + sh_ref[0]
    proj = jnp.dot(h.astype(BF16), w_ref[...], preferred_element_type=F32)
    p_ref[0] = proj[:, :pool_dim]
    cos, sa, sb = cos_ref[...], sa_ref[...], sb_ref[...]
    half = HEAD_DIM // 2

    def rope(t):
        return (t * cos + pltpu.roll(t, LANES - half, axis=1) * sa
                + pltpu.roll(t, half, axis=1) * sb)

    qscale = HEAD_DIM ** -0.5 * LOG2E
    for j in range(d // LANES):
        o = pool_dim + j * LANES
        q_ref[0, j] = (rope(proj[:, o:o + LANES]) * qscale).astype(BF16)
        o = pool_dim + d + j * LANES
        k_ref[0, j] = rope(proj[:, o:o + LANES]).astype(BF16)
        o = pool_dim + 2 * d + j * LANES
        v_ref[0, j] = proj[:, o:o + LANES].astype(BF16)


def _front(x, sc, sh, g, w_in_bf, cos, sa, sb, tm):
    b, s, d = x.shape
    in_dim = w_in_bf.shape[1]
    pool_dim = in_dim - 3 * d
    nh = d // LANES
    row = lambda i, j: (i, j, 0)
    per_b = lambda i, j: (i, 0, 0)
    const = lambda i, j: (0, 0)
    return pl.pallas_call(
        functools.partial(_front_kernel, d=d, pool_dim=pool_dim),
        grid=(b, s // tm),
        in_specs=[pl.BlockSpec((1, tm, d), row),
                  pl.BlockSpec((1, 1, d), per_b),
                  pl.BlockSpec((1, 1, d), per_b),
                  pl.BlockSpec((1, d), const),
                  pl.BlockSpec((d, in_dim), const),
                  pl.BlockSpec((tm, LANES), lambda i, j: (j, 0)),
                  pl.BlockSpec((tm, LANES), lambda i, j: (j, 0)),
                  pl.BlockSpec((tm, LANES), lambda i, j: (j, 0))],
        out_specs=[pl.BlockSpec((1, tm, pool_dim), row)]
                  + [pl.BlockSpec((1, nh, tm, LANES), lambda i, j: (i, 0, j, 0))] * 3,
        out_shape=[jax.ShapeDtypeStruct((b, s, pool_dim), F32)]
                  + [jax.ShapeDtypeStruct((b, nh, s, LANES), BF16)] * 3,
        compiler_params=_cparams(("parallel", "parallel")),
        name="front",
    )(x, sc, sh, g, w_in_bf, cos, sa, sb)


def _pool_kernel(p_ref, w_ref, scale_ref, o_ref, pad_ref, *, s):
    pos = lax.broadcasted_iota(I32, (s, LANES), 0)
    zeros = jnp.zeros((POOL_PAD, LANES), F32)
    for g, win in enumerate(POOL_WINDOWS):
        half = win // 2
        xg = p_ref[0, :, g * LANES:(g + 1) * LANES]
        pad_ref[0:POOL_PAD, :] = zeros
        pad_ref[POOL_PAD:POOL_PAD + s, :] = xg
        pad_ref[POOL_PAD + s:2 * POOL_PAD + s, :] = zeros
        wsum = pad_ref[POOL_PAD - half:POOL_PAD - half + s, :]
        for dlt in range(-half + 1, half):
            wsum = wsum + pad_ref[POOL_PAD + dlt:POOL_PAD + dlt + s, :]
        cnt = (jnp.minimum(pos + half, s) - jnp.maximum(pos - half, 0)).astype(F32)
        diff = wsum / cnt - xg
        mixed = jnp.dot(diff.astype(BF16), w_ref[g], preferred_element_type=F32)
        o_ref[0, :, g * LANES:(g + 1) * LANES] = (
            mixed * scale_ref[:, g * LANES:(g + 1) * LANES]).astype(BF16)


def _pool(p, pool_w_bf, pool_scale):
    b, s, pool_dim = p.shape
    ng = pool_w_bf.shape[0]
    return pl.pallas_call(
        functools.partial(_pool_kernel, s=s),
        grid=(b,),
        in_specs=[pl.BlockSpec((1, s, pool_dim), lambda i: (i, 0, 0)),
                  pl.BlockSpec((ng, LANES, LANES), lambda i: (0, 0, 0)),
                  pl.BlockSpec((1, pool_dim), lambda i: (0, 0))],
        out_specs=pl.BlockSpec((1, s, pool_dim), lambda i: (i, 0, 0)),
        out_shape=jax.ShapeDtypeStruct((b, s, pool_dim), BF16),
        scratch_shapes=[pltpu.VMEM((s + 2 * POOL_PAD, LANES), F32)],
        compiler_params=_cparams(("parallel",)),
        name="pool",
    )(p, pool_w_bf, pool_scale)


def _attn_kernel(lamv_ref, q_ref, k_ref, v_ref, gsub_ref, o_ref, s_scr, vext_scr,
                 *, lam_init, s, tq, tk):
    lv = lamv_ref[...]
    lam = (jnp.exp(jnp.sum(lv[0:1] * lv[1:2], axis=-1, keepdims=True))
           - jnp.exp(jnp.sum(lv[2:3] * lv[3:4], axis=-1, keepdims=True)) + lam_init)
    @pl.when(pl.program_id(2) == 0)
    def _extend_values():
        vext_scr[:, :LANES] = v_ref[0, 0]
        vext_scr[:, LANES:] = jnp.ones((s, LANES), BF16)

    q = q_ref[0, 0]
    first = lax.broadcasted_iota(I32, (1, LANES), 1) < HEAD_DIM
    zero = jnp.zeros_like(q)
    qs = (jnp.where(first, q, zero), jnp.where(first, zero, q))
    nt = (((1,), (1,)), ((), ()))
    nchunk = s // tk
    mx = [jnp.full((tq, LANES), NEG_INF, F32), jnp.full((tq, LANES), NEG_INF, F32)]
    for c in range(nchunk):
        kc = k_ref[0, 0, c * tk:(c + 1) * tk, :]
        for u in range(2):
            sc = lax.dot_general(qs[u], kc, nt, preferred_element_type=F32)
            s_scr[u, :, c * tk:(c + 1) * tk] = sc
            for jj in range(tk // LANES):
                mx[u] = jnp.maximum(mx[u], sc[:, jj * LANES:(jj + 1) * LANES])
    mx = [jnp.max(m, axis=-1, keepdims=True) for m in mx]
    acc = [jnp.zeros((tq, 2 * LANES), F32), jnp.zeros((tq, 2 * LANES), F32)]
    for c in range(nchunk):
        vc = vext_scr[c * tk:(c + 1) * tk, :]
        for u in range(2):
            p = jnp.exp2(s_scr[u, :, c * tk:(c + 1) * tk] - mx[u]).astype(BF16)
            acc[u] = acc[u] + jnp.dot(p, vc, preferred_element_type=F32)
    c1 = 1.0 / acc[0][:, LANES:LANES + 1]
    c2 = lam / acc[1][:, LANES:LANES + 1]
    o = acc[0][:, :LANES] * c1 - acc[1][:, :LANES] * c2
    o_ref[0, 0] = ((_rms(o) * gsub_ref[...]) * (1.0 - lam_init)).astype(BF16)


def _attention(q, k, v, lamv, g_sub, lam_init, tq, tk):
    b, nh, s, _ = q.shape
    return pl.pallas_call(
        functools.partial(_attn_kernel, lam_init=lam_init, s=s, tq=tq, tk=tk),
        grid=(b, nh, s // tq),
        in_specs=[pl.BlockSpec((4, HEAD_DIM), lambda i, h, j: (0, 0)),
                  pl.BlockSpec((1, 1, tq, LANES), lambda i, h, j: (i, h, j, 0)),
                  pl.BlockSpec((1, 1, s, LANES), lambda i, h, j: (i, h, 0, 0)),
                  pl.BlockSpec((1, 1, s, LANES), lambda i, h, j: (i, h, 0, 0)),
                  pl.BlockSpec((1, LANES), lambda i, h, j: (0, 0))],
        out_specs=pl.BlockSpec((1, 1, tq, LANES), lambda i, h, j: (i, h, j, 0)),
        out_shape=jax.ShapeDtypeStruct((b, nh, s, LANES), BF16),
        scratch_shapes=[pltpu.VMEM((2, tq, s), F32), pltpu.VMEM((s, 2 * LANES), BF16)],
        compiler_params=_cparams(("parallel", "parallel", "arbitrary")),
        name="attn",
    )(lamv, q, k, v, g_sub)


def _merge_kernel(x_ref, mixed_ref, att_ref, sc1_ref, sh1_ref, gt1_ref, sc2_ref, sh2_ref,
                  g1_ref, g2_ref, wpool_ref, watt_ref, wgate_ref, bgate_ref, wout_ref,
                  x1_ref, h2_ref, *, d):
    x = x_ref[0]
    h = ((_rms(x) * g1_ref[...]) * (1.0 + sc1_ref[0]) + sh1_ref[0]).astype(BF16)
    gates = jax.nn.sigmoid(jnp.dot(h, wgate_ref[...], preferred_element_type=F32) + bgate_ref[...])
    y_pool = jnp.dot(mixed_ref[0], wpool_ref[...], preferred_element_type=F32)
    att = jnp.concatenate([att_ref[0, hd] for hd in range(d // LANES)], axis=1)
    y_att = jnp.dot(att, watt_ref[...], preferred_element_type=F32)
    merged = gates[:, :d] * y_pool + gates[:, d:] * y_att
    x1 = x + gt1_ref[0] * jnp.dot(merged.astype(BF16), wout_ref[...], preferred_element_type=F32)
    x1_ref[0] = x1
    h2 = (_rms(x1) * g2_ref[...]) * (1.0 + sc2_ref[0]) + sh2_ref[0]
    h2_ref[0] = h2.astype(BF16)


def _merge(x, mixed, att, mods, g1, g2, wpool, watt, wgate, bgate, wout, tm):
    b, s, d = x.shape
    pool_dim = mixed.shape[-1]
    row = lambda i, j: (i, j, 0)
    per_b = lambda i, j: (i, 0, 0)
    const = lambda i, j: (0, 0)
    sc1, sh1, gt1, sc2, sh2 = mods
    return pl.pallas_call(
        functools.partial(_merge_kernel, d=d),
        grid=(b, s // tm),
        in_specs=[pl.BlockSpec((1, tm, d), row),
                  pl.BlockSpec((1, tm, pool_dim), row),
                  pl.BlockSpec((1, d // LANES, tm, LANES), lambda i, j: (i, 0, j, 0))]
                 + [pl.BlockSpec((1, 1, d), per_b)] * 5
                 + [pl.BlockSpec((1, d), const), pl.BlockSpec((1, d), const),
                    pl.BlockSpec((pool_dim, d), const), pl.BlockSpec((d, d), const),
                    pl.BlockSpec((d, 2 * d), const), pl.BlockSpec((1, 2 * d), const),
                    pl.BlockSpec((d, d), const)],
        out_specs=[pl.BlockSpec((1, tm, d), row), pl.BlockSpec((1, tm, d), row)],
        out_shape=[jax.ShapeDtypeStruct((b, s, d), F32), jax.ShapeDtypeStruct((b, s, d), BF16)],
        compiler_params=_cparams(("parallel", "parallel")),
        name="merge",
    )(x, mixed, att, sc1, sh1, gt1, sc2, sh2, g1, g2, wpool, watt, wgate, bgate, wout)


ROUTE_TOKENS = 8 * LANES
LAY_PITCH = LANES + 8


def _sort16_network():
    n, pairs, p = TOPK, [], 1
    while p < n:
        k = p
        while k >= 1:
            for j in range(k % p, n - k, 2 * k):
                for i in range(min(k, n - j - k)):
                    if (i + j) // (2 * p) == (i + j + k) // (2 * p):
                        pairs.append((i + j, i + j + k))
            k //= 2
        p *= 2
    return tuple(pairs)


SORT16 = _sort16_network()


def _swap_mask(a, b):
    (av, ai), (bv, bi) = a, b
    return (bv > av) | ((bv == av) & (bi < ai))


def _order(a, b):
    sw = _swap_mask(a, b)
    first = (jnp.maximum(a[0], b[0]), jnp.where(sw, b[1], a[1]))
    second = (jnp.minimum(a[0], b[0]), jnp.where(sw, a[1], b[1]))
    return first, second


def _first_of(a, b):
    return jnp.maximum(a[0], b[0]), jnp.where(_swap_mask(a, b), b[1], a[1])


def _sort16(items):
    items = list(items)
    for i, j in SORT16:
        items[i], items[j] = _order(items[i], items[j])
    return items


def _merge_top16(a, b):
    c = [_first_of(a[i], b[TOPK - 1 - i]) for i in range(TOPK)]
    stride = TOPK // 2
    while stride:
        for i in range(TOPK):
            if not i & stride:
                c[i], c[i + stride] = _order(c[i], c[i + stride])
        stride //= 2
    return c


def _route_kernel(h2_ref, wqt_ref, keys_ref, e1_ref, e2_ref, gate_ref,
                  qt_scr, lay_scr, tv_scr, ti_scr):
    nblk = ROUTE_TOKENS // LANES
    qt = lax.dot_general(wqt_ref[...], h2_ref[...], (((1,), (1,)), ((), ())),
                         preferred_element_type=F32)
    qt_scr[...] = qt.astype(BF16)

    def set_body(si, carry):
        r0 = pl.multiple_of(si * HEAD_DIM, HEAD_DIM)
        st = jnp.dot(keys_ref[si], qt_scr[pl.ds(r0, HEAD_DIM), :], preferred_element_type=F32)
        for c in range(nblk):
            lay_scr[c * LAY_PITCH:c * LAY_PITCH + LANES, :] = st[:, c * LANES:(c + 1) * LANES]
        lists = []
        for g in range(LANES // TOPK):
            items = []
            for k in range(g * TOPK, (g + 1) * TOPK):
                items.append((lay_scr[pl.ds(k, nblk, stride=LAY_PITCH), :],
                              jnp.full((nblk, LANES), k, I32)))
            lists.append(_sort16(items))
        while len(lists) > 1:
            lists = [_merge_top16(lists[2 * i], lists[2 * i + 1]) for i in range(len(lists) // 2)]
        for a, (v, idx) in enumerate(lists[0]):
            tv_scr[si * TOPK + a] = v
            ti_scr[si * TOPK + a] = idx
        return carry

    lax.fori_loop(0, 2 * N_HEADS, set_body, 0)

    def head_body(h, carry):
        b1 = 2 * h * TOPK
        b2 = b1 + TOPK
        cv, ce = [], []
        for a, nb in CAND_ROWS:
            va = tv_scr[b1 + a]
            ea = ti_scr[b1 + a] * LANES
            for b in range(nb):
                cv.append(va + tv_scr[b2 + b])
                ce.append(ea + ti_scr[b2 + b])
        best_s, best_e = [], []
        for _ in range(TOPK):
            nodes = list(zip(cv, ce))
            while len(nodes) > 1:
                nxt = []
                for i in range(0, len(nodes) - 1, 2):
                    (lv, le), (rv, re) = nodes[i], nodes[i + 1]
                    nxt.append((jnp.maximum(lv, rv), jnp.where(rv > lv, re, le)))
                if len(nodes) % 2:
                    nxt.append(nodes[-1])
                nodes = nxt
            m, e = nodes[0]
            best_s.append(m)
            best_e.append(e)
            cv = [jnp.where(c_e == e, NEG_INF, c_v) for c_v, c_e in zip(cv, ce)]
        ex = [jnp.exp(b - best_s[0]) for b in best_s]
        z = ex[0]
        for t in ex[1:]:
            z = z + t
        for k in range(TOPK):
            e1_ref[h * TOPK + k] = lax.shift_right_logical(best_e[k], 7)
            e2_ref[h * TOPK + k] = best_e[k] & (LANES - 1)
            gate_ref[h * TOPK + k] = ex[k] / z
        return carry

    lax.fori_loop(0, N_HEADS, head_body, 0)


def _route(h2_flat, wqt_bf, keys_bf):
    t, d = h2_flat.shape
    assert t % ROUTE_TOKENS == 0
    nslot = N_HEADS * TOPK
    nblk = ROUTE_TOKENS // LANES
    keys2 = keys_bf.reshape(2 * N_HEADS, LANES, HEAD_DIM)
    out = pl.BlockSpec((nslot, nblk, LANES), lambda i: (0, i, 0))
    outs = pl.pallas_call(
        _route_kernel,
        grid=(t // ROUTE_TOKENS,),
        in_specs=[pl.BlockSpec((ROUTE_TOKENS, d), lambda i: (i, 0)),
                  pl.BlockSpec(wqt_bf.shape, lambda i: (0, 0)),
                  pl.BlockSpec(keys2.shape, lambda i: (0, 0, 0))],
        out_specs=[out, out, out],
        out_shape=[jax.ShapeDtypeStruct((nslot, t // LANES, LANES), I32),
                   jax.ShapeDtypeStruct((nslot, t // LANES, LANES), I32),
                   jax.ShapeDtypeStruct((nslot, t // LANES, LANES), F32)],
        scratch_shapes=[pltpu.VMEM((wqt_bf.shape[0], ROUTE_TOKENS), BF16),
                        pltpu.VMEM((nblk * LAY_PITCH, LANES), F32),
                        pltpu.VMEM((2 * N_HEADS * TOPK, nblk, LANES), F32),
                        pltpu.VMEM((2 * N_HEADS * TOPK, nblk, LANES), I32)],
        compiler_params=_cparams(("parallel",)),
        name="route",
    )(h2_flat, wqt_bf, keys2)
    return [o.reshape(nslot, t).T for o in outs]


GATE_ROWS = LANES // 2
GATE_PITCH = GATE_ROWS + 4
SUBLANES = 8
BUILD_GROUP = 64
HI_MASK = -65536
PEER_CHUNK = 2048
PEER_TOKENS = 512


def _peer_kernel(h2_ref, e1_ref, e2_ref, gate_ref, ut0_ref, v0_ref, ut_ref, v_ref, x1_ref,
                 gt2_ref, gf_ref, y_ref, gmat_scr, w_scr, *, tp, ec):
    c = pl.program_id(1)
    nsub = ec // LANES

    @pl.when(c == 0)
    def _build_gate_matrices():
        y_ref[...] = jnp.zeros_like(y_ref)
        key_id = lax.broadcasted_iota(I32, (LANES, LANES), 0)

        def body(tb, carry):
            for sub in range(BUILD_GROUP // SUBLANES):
                base = pl.multiple_of(tb * BUILD_GROUP + sub * SUBLANES, SUBLANES)
                r1 = e1_ref[pl.ds(base, SUBLANES), :]
                r2 = e2_ref[pl.ds(base, SUBLANES), :]
                gg = gate_ref[pl.ds(base, SUBLANES), :]
                for i in range(SUBLANES):
                    a = jnp.where(key_id == r1[i:i + 1, :], gg[i:i + 1, :], 0.0).astype(BF16)
                    b = jnp.where(key_id == r2[i:i + 1, :], 1.0, 0.0).astype(BF16)
                    gm = lax.dot_general(a, b, (((1,), (1,)), ((), ())),
                                         preferred_element_type=F32)
                    bits = lax.bitcast_convert_type(gm, I32) + 32768
                    packed = ((bits[:GATE_ROWS] & HI_MASK)
                              | lax.shift_right_logical(bits[GATE_ROWS:], 16))
                    gmat_scr[pl.ds((base + i) * GATE_PITCH, GATE_ROWS), :] = packed
            return carry

        lax.fori_loop(0, tp // BUILD_GROUP, body, 0)

    def evaluate_chunk(ut_ref, v_ref):
        h2 = h2_ref[...]
        for jb in range(ec // MXU_DEPTH):
            lo = jb * MXU_DEPTH
            s = jnp.dot(h2, ut_ref[0, :, lo:lo + MXU_DEPTH], preferred_element_type=F32)
            for u in range(MXU_DEPTH // LANES):
                k1 = c * nsub + jb * (MXU_DEPTH // LANES) + u
                word = gmat_scr[pl.ds(k1 % GATE_ROWS, tp, stride=GATE_PITCH), :]
                shift = (k1 // GATE_ROWS) * 16
                gi = lax.bitcast_convert_type(lax.shift_left(word, shift) & HI_MASK, F32)
                su = s[:, u * LANES:(u + 1) * LANES]
                w_scr[:, lo + u * LANES:lo + (u + 1) * LANES] = (
                    (0.5 * su) * (1.0 + lax.erf(su * INV_SQRT2)) * gi).astype(BF16)
        y_ref[...] += jnp.dot(w_scr[...], v_ref[...], preferred_element_type=F32)

    @pl.when(c == 0)
    def _resident_chunk():
        evaluate_chunk(ut0_ref, v0_ref)

    @pl.when(c > 0)
    def _streamed_chunk():
        evaluate_chunk(ut_ref, v_ref)

    @pl.when(c == pl.num_programs(1) - 1)
    def _finish():
        xo = x1_ref[...] + gt2_ref[0] * y_ref[...]
        y_ref[...] = _rms(xo) * gf_ref[...]


def _peer(h2_flat, e1, e2, gate, ut_bf, v_bf, x1_flat, gt2, g_final, tokens_per_batch, tp):
    t, d = h2_flat.shape
    ec = ut_bf.shape[2]
    nc = v_bf.shape[0] // ec
    nslot = e1.shape[1]
    tiles_per_batch = tokens_per_batch // tp
    tok = lambda i, c: (i, 0)
    once = pl.Buffered(1)
    return pl.pallas_call(
        functools.partial(_peer_kernel, tp=tp, ec=ec),
        grid=(t // tp, nc),
        in_specs=[pl.BlockSpec((tp, d), tok),
                  pl.BlockSpec((tp, nslot), tok),
                  pl.BlockSpec((tp, nslot), tok),
                  pl.BlockSpec((tp, nslot), tok),
                  pl.BlockSpec((1, d, ec), lambda i, c: (0, 0, 0), pipeline_mode=once),
                  pl.BlockSpec((ec, d), lambda i, c: (0, 0), pipeline_mode=once),
                  pl.BlockSpec((1, d, ec), lambda i, c: (jnp.maximum(c, 1), 0, 0)),
                  pl.BlockSpec((ec, d), lambda i, c: (jnp.maximum(c, 1), 0)),
                  pl.BlockSpec((tp, d), tok, pipeline_mode=once),
                  pl.BlockSpec((1, 1, d), lambda i, c: (i // tiles_per_batch, 0, 0)),
                  pl.BlockSpec((1, d), lambda i, c: (0, 0))],
        out_specs=pl.BlockSpec((tp, d), tok),
        out_shape=jax.ShapeDtypeStruct((t, d), F32),
        scratch_shapes=[pltpu.VMEM((tp * GATE_PITCH, LANES), I32),
                        pltpu.VMEM((tp, ec), BF16)],
        compiler_params=_cparams(("parallel", "arbitrary")),
        name="peer",
    )(h2_flat, e1, e2, gate, ut_bf, v_bf, ut_bf, v_bf, x1_flat, gt2, g_final)


def _rope_tables(s):
    inv_freq = 1.0 / (ROPE_THETA ** (jnp.arange(0, HEAD_DIM, 2, dtype=F32) / HEAD_DIM))
    ang = jnp.arange(s, dtype=F32)[:, None] * inv_freq[None, :]
    ang = jnp.concatenate([ang, ang, ang, ang], axis=-1)
    cos, sin = jnp.cos(ang), jnp.sin(ang)
    low = (jnp.arange(LANES) % HEAD_DIM) < (HEAD_DIM // 2)
    return cos, jnp.where(low, -sin, 0.0), jnp.where(low, 0.0, sin)


def _pick_tile(n, want):
    t = min(n, want)
    while n % t:
        t //= 2
    return t


def _trunk(x, mod, wts, lam_init):
    b, s, d = x.shape
    sh1, sc1, gt1, sh2, sc2, gt2 = [m.reshape(b, 1, d) for m in jnp.split(mod, 6, axis=-1)]
    cos, sa, sb = _rope_tables(s)
    p, q, k, v = _front(x, sc1, sh1, wts["g_norm1"], wts["w_in"], cos, sa, sb, _pick_tile(s, 512))
    mixed = _pool(p, wts["pool_w"], wts["pool_scale"])
    att = _attention(q, k, v, wts["lamv"], wts["g_sub"], lam_init, _pick_tile(s, 512),
                     _pick_tile(s, 512))
    x1, h2 = _merge(x, mixed, att, (sc1, sh1, gt1, sc2, sh2), wts["g_norm1"], wts["g_norm2"],
                    wts["w_pool_out"], wts["w_att_out"], wts["w_gate"], wts["b_gate"],
                    wts["w_out"], _pick_tile(s, 512))
    t = b * s
    h2f = h2.reshape(t, d)
    e1, e2, gate = _route(h2f, wts["w_peer_qt"], wts["peer_keys"])
    y = _peer(h2f, e1, e2, gate, wts["peer_ut"], wts["peer_v"], x1.reshape(t, d),
              gt2, wts["g_final"], s, _pick_tile(s, PEER_TOKENS))
    return y.reshape(b, s, d)


def kernel(x_prompt, x_sample, c_prompt, c_sample, w_ada, b_ada, g_norm1, w_in, pool_w, pool_scale,
           w_pool_out, lam_q1, lam_k1, lam_q2, lam_k2, g_sub, w_att_out, w_gate, b_gate, w_out,
           g_norm2, w_peer_q, peer_keys, peer_u, peer_v, g_final):
    depth = w_ada.shape[0]
    assert depth == 1, "single-layer trunk"
    l = 0
    lam_init = 0.8 - 0.6 * math.exp(-0.3 * l)
    d = x_prompt.shape[-1]
    wts = {
        "g_norm1": g_norm1[l][None, :], "g_norm2": g_norm2[l][None, :], "g_final": g_final[None, :],
        "w_in": w_in[l].astype(BF16),
        "pool_w": pool_w[l].astype(BF16), "pool_scale": pool_scale[l][None, :],
        "w_pool_out": w_pool_out[l].astype(BF16),
        "lamv": jnp.stack([lam_q1[l], lam_k1[l], lam_q2[l], lam_k2[l]]).astype(F32),
        "g_sub": g_sub[l][None, :],
        "w_att_out": w_att_out[l].astype(BF16),
        "w_gate": w_gate[l].astype(BF16), "b_gate": b_gate[l][None, :],
        "w_out": w_out[l].astype(BF16),
        "w_peer_qt": w_peer_q[l].T.astype(BF16),
        "peer_keys": peer_keys[l].astype(BF16),
        "peer_ut": peer_u[l].astype(BF16).reshape(-1, PEER_CHUNK, d).transpose(0, 2, 1),
        "peer_v": peer_v[l].astype(BF16),
    }
    nb = x_prompt.shape[0]
    mod = _ada(jnp.concatenate([c_prompt, c_sample], axis=0), w_ada[l].astype(BF16), b_ada[l][None, :])
    y_prompt = _trunk(x_prompt, mod[:nb], wts, lam_init)
    y_sample = _trunk(x_sample, mod[nb:], wts, lam_init)
    return (y_prompt, y_sample)
```

```python
import functools
import math

import jax
import jax.numpy as jnp
from jax import lax
from jax.experimental import pallas as pl
from jax.experimental.pallas import tpu as pltpu

F32 = jnp.float32
BF16 = jnp.bfloat16
I32 = jnp.int32

NORM_EPS = 1e-6
ROPE_THETA = 10000.0
POOL_WINDOWS = (2, 4, 8, 16)
POOL_PAD = 16
N_HEADS = 8
HEAD_DIM = 64
LANES = 128
MXU_DEPTH = 256
TOPK = 16
CAND_ROWS = tuple((a, TOPK // (a + 1)) for a in range(TOPK))
VMEM_LIMIT = 56 * 1024 * 1024
ATTN_SCORE_BYTES = 16 * 1024 * 1024
INV_SQRT2 = 1.0 / math.sqrt(2.0)
LOG2E = 1.0 / math.log(2.0)
NEG_INF = float("-inf")


def _cparams(sem):
    return pltpu.CompilerParams(dimension_semantics=sem, vmem_limit_bytes=VMEM_LIMIT)


def _rms(x):
    return x * lax.rsqrt(jnp.mean(x * x, axis=-1, keepdims=True) + NORM_EPS)


def _ada_kernel(c_ref, w_ref, b_ref, o_ref):
    c = c_ref[...]
    a = (c * jax.nn.sigmoid(c)).astype(BF16)
    o_ref[...] = jnp.dot(a, w_ref[...], preferred_element_type=F32) + b_ref[...]


def _ada(c, w_bf, b):
    n, d = c.shape
    nout = w_bf.shape[1]
    return pl.pallas_call(
        _ada_kernel,
        grid=(nout // d,),
        in_specs=[pl.BlockSpec((n, d), lambda j: (0, 0)),
                  pl.BlockSpec((d, d), lambda j: (0, j)),
                  pl.BlockSpec((1, d), lambda j: (0, j))],
        out_specs=pl.BlockSpec((n, d), lambda j: (0, j)),
        out_shape=jax.ShapeDtypeStruct((n, nout), F32),
        compiler_params=_cparams(("parallel",)),
        name="ada",
    )(c, w_bf, b)


def _front_kernel(x_ref, sc_ref, sh_ref, g_ref, w_ref, cos_ref, sa_ref, sb_ref,
                  p_ref, q_ref, k_ref, v_ref, *, d, pool_dim):
    x = x_ref[0]
    h = (_rms(x) * g_ref[...]) * (1.0 + sc_ref[0]) + sh_ref[0]
    proj = jnp.dot(h.astype(BF16), w_ref[...], preferred_element_type=F32)
    p_ref[0] = proj[:, :pool_dim]
    cos, sa, sb = cos_ref[...], sa_ref[...], sb_ref[...]
    half = HEAD_DIM // 2

    def rope(t):
        return (t * cos + pltpu.roll(t, LANES - half, axis=1) * sa
                + pltpu.roll(t, half, axis=1) * sb)

    qscale = HEAD_DIM ** -0.5 * LOG2E
    for j in range(d // LANES):
        o = pool_dim + j * LANES
        q_ref[0, j] = (rope(proj[:, o:o + LANES]) * qscale).astype(BF16)
        o = pool_dim + d + j * LANES
        k_ref[0, j] = rope(proj[:, o:o + LANES]).astype(BF16)
        o = pool_dim + 2 * d + j * LANES
        v_ref[0, j] = proj[:, o:o + LANES].astype(BF16)


def _front(x, sc, sh, g, w_in_bf, cos, sa, sb, tm):
    b, s, d = x.shape
    in_dim = w_in_bf.shape[1]
    pool_dim = in_dim - 3 * d
    nh = d // LANES
    row = lambda i, j: (i, j, 0)
    per_b = lambda i, j: (i, 0, 0)
    const = lambda i, j: (0, 0)
    return pl.pallas_call(
        functools.partial(_front_kernel, d=d, pool_dim=pool_dim),
        grid=(b, s // tm),
        in_specs=[pl.BlockSpec((1, tm, d), row),
                  pl.BlockSpec((1, 1, d), per_b),
                  pl.BlockSpec((1, 1, d), per_b),
                  pl.BlockSpec((1, d), const),
                  pl.BlockSpec((d, in_dim), const),
                  pl.BlockSpec((tm, LANES), lambda i, j: (j, 0)),
                  pl.BlockSpec((tm, LANES), lambda i, j: (j, 0)),
                  pl.BlockSpec((tm, LANES), lambda i, j: (j, 0))],
        out_specs=[pl.BlockSpec((1, tm, pool_dim), row)]
                  + [pl.BlockSpec((1, nh, tm, LANES), lambda i, j: (i, 0, j, 0))] * 3,
        out_shape=[jax.ShapeDtypeStruct((b, s, pool_dim), F32)]
                  + [jax.ShapeDtypeStruct((b, nh, s, LANES), BF16)] * 3,
        compiler_params=_cparams(("parallel", "parallel")),
        name="front",
    )(x, sc, sh, g, w_in_bf, cos, sa, sb)


def _pool_kernel(p_ref, w_ref, scale_ref, o_ref, pad_ref, *, s):
    pos = lax.broadcasted_iota(I32, (s, LANES), 0)
    zeros = jnp.zeros((POOL_PAD, LANES), F32)
    for g, win in enumerate(POOL_WINDOWS):
        half = win // 2
        xg = p_ref[0, :, g * LANES:(g + 1) * LANES]
        pad_ref[0:POOL_PAD, :] = zeros
        pad_ref[POOL_PAD:POOL_PAD + s, :] = xg
        pad_ref[POOL_PAD + s:2 * POOL_PAD + s, :] = zeros
        wsum = pad_ref[POOL_PAD - half:POOL_PAD - half + s, :]
        for dlt in range(-half + 1, half):
            wsum = wsum + pad_ref[POOL_PAD + dlt:POOL_PAD + dlt + s, :]
        cnt = (jnp.minimum(pos + half, s) - jnp.maximum(pos - half, 0)).astype(F32)
        diff = wsum / cnt - xg
        mixed = jnp.dot(diff.astype(BF16), w_ref[g], preferred_element_type=F32)
        o_ref[0, :, g * LANES:(g + 1) * LANES] = (
            mixed * scale_ref[:, g * LANES:(g + 1) * LANES]).astype(BF16)


def _pool(p, pool_w_bf, pool_scale):
    b, s, pool_dim = p.shape
    ng = pool_w_bf.shape[0]
    return pl.pallas_call(
        functools.partial(_pool_kernel, s=s),
        grid=(b,),
        in_specs=[pl.BlockSpec((1, s, pool_dim), lambda i: (i, 0, 0)),
                  pl.BlockSpec((ng, LANES, LANES), lambda i: (0, 0, 0)),
                  pl.BlockSpec((1, pool_dim), lambda i: (0, 0))],
        out_specs=pl.BlockSpec((1, s, pool_dim), lambda i: (i, 0, 0)),
        out_shape=jax.ShapeDtypeStruct((b, s, pool_dim), BF16),
        scratch_shapes=[pltpu.VMEM((s + 2 * POOL_PAD, LANES), F32)],
        compiler_params=_cparams(("parallel",)),
        name="pool",
    )(p, pool_w_bf, pool_scale)


def _attn_kernel(lamv_ref, q_ref, k_ref, v_ref, gsub_ref, o_ref, s_scr, vext_scr,
                 *, lam_init, s, tq, tk):
    lv = lamv_ref[...]
    lam = (jnp.exp(jnp.sum(lv[0:1] * lv[1:2], axis=-1, keepdims=True))
           - jnp.exp(jnp.sum(lv[2:3] * lv[3:4], axis=-1, keepdims=True)) + lam_init)
    @pl.when(pl.program_id(2) == 0)
    def _extend_values():
        vext_scr[:, :LANES] = v_ref[0, 0]
        vext_scr[:, LANES:] = jnp.ones((s, LANES), BF16)

    q = q_ref[0, 0]
    first = lax.broadcasted_iota(I32, (1, LANES), 1) < HEAD_DIM
    zero = jnp.zeros_like(q)
    qs = (jnp.where(first, q, zero), jnp.where(first, zero, q))
    nt = (((1,), (1,)), ((), ()))
    nchunk = s // tk
    mx = [jnp.full((tq, LANES), NEG_INF, F32), jnp.full((tq, LANES), NEG_INF, F32)]
    for c in range(nchunk):
        kc = k_ref[0, 0, c * tk:(c + 1) * tk, :]
        for u in range(2):
            sc = lax.dot_general(qs[u], kc, nt, preferred_element_type=F32)
            s_scr[u, :, c * tk:(c + 1) * tk] = sc
            for jj in range(tk // LANES):
                mx[u] = jnp.maximum(mx[u], sc[:, jj * LANES:(jj + 1) * LANES])
    mx = [jnp.max(m, axis=-1, keepdims=True) for m in mx]
    acc = [jnp.zeros((tq, 2 * LANES), F32), jnp.zeros((tq, 2 * LANES), F32)]
    for c in range(nchunk):
        vc = vext_scr[c * tk:(c + 1) * tk, :]
        for u in range(2):
            p = jnp.exp2(s_scr[u, :, c * tk:(c + 1) * tk] - mx[u]).astype(BF16)
            acc[u] = acc[u] + jnp.dot(p, vc, preferred_element_type=F32)
    c1 = 1.0 / acc[0][:, LANES:LANES + 1]
    c2 = lam / acc[1][:, LANES:LANES + 1]
    o = acc[0][:, :LANES] * c1 - acc[1][:, :LANES] * c2
    o_ref[0, 0] = ((_rms(o) * gsub_ref[...]) * (1.0 - lam_init)).astype(BF16)


def _attention(q, k, v, lamv, g_sub, lam_init, tq, tk):
    b, nh, s, _ = q.shape
    return pl.pallas_call(
        functools.partial(_attn_kernel, lam_init=lam_init, s=s, tq=tq, tk=tk),
        grid=(b, nh, s // tq),
        in_specs=[pl.BlockSpec((4, HEAD_DIM), lambda i, h, j: (0, 0)),
                  pl.BlockSpec((1, 1, tq, LANES), lambda i, h, j: (i, h, j, 0)),
                  pl.BlockSpec((1, 1, s, LANES), lambda i, h, j: (i, h, 0, 0)),
                  pl.BlockSpec((1, 1, s, LANES), lambda i, h, j: (i, h, 0, 0)),
                  pl.BlockSpec((1, LANES), lambda i, h, j: (0, 0))],
        out_specs=pl.BlockSpec((1, 1, tq, LANES), lambda i, h, j: (i, h, j, 0)),
        out_shape=jax.ShapeDtypeStruct((b, nh, s, LANES), BF16),
        scratch_shapes=[pltpu.VMEM((2, tq, s), F32), pltpu.VMEM((s, 2 * LANES), BF16)],
        compiler_params=_cparams(("parallel", "parallel", "arbitrary")),
        name="attn",
    )(lamv, q, k, v, g_sub)


def _merge_kernel(x_ref, mixed_ref, att_ref, sc1_ref, sh1_ref, gt1_ref, sc2_ref, sh2_ref,
                  g1_ref, g2_ref, wpool_ref, watt_ref, wgate_ref, bgate_ref, wout_ref,
                  x1_ref, h2_ref, *, d):
    x = x_ref[0]
    h = ((_rms(x) * g1_ref[...]) * (1.0 + sc1_ref[0]) + sh1_ref[0]).astype(BF16)
    gates = jax.nn.sigmoid(jnp.dot(h, wgate_ref[...], preferred_element_type=F32) + bgate_ref[...])
    y_pool = jnp.dot(mixed_ref[0], wpool_ref[...], preferred_element_type=F32)
    att = jnp.concatenate([att_ref[0, hd] for hd in range(d // LANES)], axis=1)
    y_att = jnp.dot(att, watt_ref[...], preferred_element_type=F32)
    merged = gates[:, :d] * y_pool + gates[:, d:] * y_att
    x1 = x + gt1_ref[0] * jnp.dot(merged.astype(BF16), wout_ref[...], preferred_element_type=F32)
    x1_ref[0] = x1
    h2 = (_rms(x1) * g2_ref[...]) * (1.0 + sc2_ref[0]) + sh2_ref[0]
    h2_ref[0] = h2.astype(BF16)


def _merge(x, mixed, att, mods, g1, g2, wpool, watt, wgate, bgate, wout, tm):
    b, s, d = x.shape
    pool_dim = mixed.shape[-1]
    row = lambda i, j: (i, j, 0)
    per_b = lambda i, j: (i, 0, 0)
    const = lambda i, j: (0, 0)
    sc1, sh1, gt1, sc2, sh2 = mods
    return pl.pallas_call(
        functools.partial(_merge_kernel, d=d),
        grid=(b, s // tm),
        in_specs=[pl.BlockSpec((1, tm, d), row),
                  pl.BlockSpec((1, tm, pool_dim), row),
                  pl.BlockSpec((1, d // LANES, tm, LANES), lambda i, j: (i, 0, j, 0))]
                 + [pl.BlockSpec((1, 1, d), per_b)] * 5
                 + [pl.BlockSpec((1, d), const), pl.BlockSpec((1, d), const),
                    pl.BlockSpec((pool_dim, d), const), pl.BlockSpec((d, d), const),
                    pl.BlockSpec((d, 2 * d), const), pl.BlockSpec((1, 2 * d), const),
                    pl.BlockSpec((d, d), const)],
        out_specs=[pl.BlockSpec((1, tm, d), row), pl.BlockSpec((1, tm, d), row)],
        out_shape=[jax.ShapeDtypeStruct((b, s, d), F32), jax.ShapeDtypeStruct((b, s, d), BF16)],
        compiler_params=_cparams(("parallel", "parallel")),
        name="merge",
    )(x, mixed, att, sc1, sh1, gt1, sc2, sh2, g1, g2, wpool, watt, wgate, bgate, wout)


ROUTE_TOKENS = 8 * LANES
LAY_PITCH = LANES + 8


def _sort16_network():
    n, pairs, p = TOPK, [], 1
    while p < n:
        k = p
        while k >= 1:
            for j in range(k % p, n - k, 2 * k):
                for i in range(min(k, n - j - k)):
                    if (i + j) // (2 * p) == (i + j + k) // (2 * p):
                        pairs.append((i + j, i + j + k))
            k //= 2
        p *= 2
    return tuple(pairs)


SORT16 = _sort16_network()


def _swap_mask(a, b):
    (av, ai), (bv, bi) = a, b
    return (bv > av) | ((bv == av) & (bi < ai))


def _order(a, b):
    sw = _swap_mask(a, b)
    first = (jnp.maximum(a[0], b[0]), jnp.where(sw, b[1], a[1]))
    second = (jnp.minimum(a[0], b[0]), jnp.where(sw, a[1], b[1]))
    return first, second


def _first_of(a, b):
    return jnp.maximum(a[0], b[0]), jnp.where(_swap_mask(a, b), b[1], a[1])


def _sort16(items):
    items = list(items)
    for i, j in SORT16:
        items[i], items[j] = _order(items[i], items[j])
    return items


def _merge_top16(a, b):
    c = [_first_of(a[i], b[TOPK - 1 - i]) for i in range(TOPK)]
    stride = TOPK // 2
    while stride:
        for i in range(TOPK):
            if not i & stride:
                c[i], c[i + stride] = _order(c[i], c[i + stride])
        stride //= 2
    return c


def _route_kernel(h2_ref, wqt_ref, keys_ref, e1_ref, e2_ref, gate_ref,
                  qt_scr, lay_scr, tv_scr, ti_scr):
    nblk = ROUTE_TOKENS // LANES
    qt = lax.dot_general(wqt_ref[...], h2_ref[...], (((1,), (1,)), ((), ())),
                         preferred_element_type=F32)
    qt_scr[...] = qt.astype(BF16)

    def set_body(h, carry):
        for part in range(2):
            si = 2 * h + part
            r0 = pl.multiple_of(si * HEAD_DIM, HEAD_DIM)
            st = jnp.dot(keys_ref[si], qt_scr[pl.ds(r0, HEAD_DIM), :],
                         preferred_element_type=F32)
            for c in range(nblk):
                row = (part * nblk + c) * LAY_PITCH
                lay_scr[row:row + LANES, :] = st[:, c * LANES:(c + 1) * LANES]
        lists = []
        for g in range(LANES // TOPK):
            items = []
            for k in range(g * TOPK, (g + 1) * TOPK):
                v = jnp.stack([lay_scr[pl.ds(part * nblk * LAY_PITCH + k, nblk, stride=LAY_PITCH), :]
                               for part in range(2)])
                items.append((v, jnp.full((2, nblk, LANES), k, I32)))
            lists.append(_sort16(items))
        while len(lists) > 1:
            lists = [_merge_top16(lists[2 * i], lists[2 * i + 1]) for i in range(len(lists) // 2)]
        for a, (v, idx) in enumerate(lists[0]):
            for part in range(2):
                tv_scr[(2 * h + part) * TOPK + a] = v[part]
                ti_scr[(2 * h + part) * TOPK + a] = idx[part]
        return carry

    lax.fori_loop(0, N_HEADS, set_body, 0)

    def head_body(h, carry):
        b1 = 2 * h * TOPK
        b2 = b1 + TOPK
        cv, ce = [], []
        for a, nb in CAND_ROWS:
            va = tv_scr[b1 + a]
            ea = ti_scr[b1 + a] * LANES
            for b in range(nb):
                cv.append(va + tv_scr[b2 + b])
                ce.append(ea + ti_scr[b2 + b])
        best_s, best_e = [], []
        for _ in range(TOPK):
            nodes = list(zip(cv, ce))
            while len(nodes) > 1:
                nxt = []
                for i in range(0, len(nodes) - 1, 2):
                    (lv, le), (rv, re) = nodes[i], nodes[i + 1]
                    nxt.append((jnp.maximum(lv, rv), jnp.where(rv > lv, re, le)))
                if len(nodes) % 2:
                    nxt.append(nodes[-1])
                nodes = nxt
            m, e = nodes[0]
            best_s.append(m)
            best_e.append(e)
            cv = [jnp.where(c_e == e, NEG_INF, c_v) for c_v, c_e in zip(cv, ce)]
        ex = [jnp.exp(b - best_s[0]) for b in best_s]
        z = ex[0]
        for t in ex[1:]:
            z = z + t
        for k in range(TOPK):
            e1_ref[h * TOPK + k] = lax.shift_right_logical(best_e[k], 7)
            e2_ref[h * TOPK + k] = best_e[k] & (LANES - 1)
            gate_ref[h * TOPK + k] = ex[k] / z
        return carry

    lax.fori_loop(0, N_HEADS, head_body, 0)


def _route(h2_flat, wqt_bf, keys_bf):
    t, d = h2_flat.shape
    assert t % ROUTE_TOKENS == 0
    nslot = N_HEADS * TOPK
    nblk = ROUTE_TOKENS // LANES
    keys2 = keys_bf.reshape(2 * N_HEADS, LANES, HEAD_DIM)
    out = pl.BlockSpec((nslot, nblk, LANES), lambda i: (0, i, 0))
    outs = pl.pallas_call(
        _route_kernel,
        grid=(t // ROUTE_TOKENS,),
        in_specs=[pl.BlockSpec((ROUTE_TOKENS, d), lambda i: (i, 0)),
                  pl.BlockSpec(wqt_bf.shape, lambda i: (0, 0)),
                  pl.BlockSpec(keys2.shape, lambda i: (0, 0, 0))],
        out_specs=[out, out, out],
        out_shape=[jax.ShapeDtypeStruct((nslot, t // LANES, LANES), I32),
                   jax.ShapeDtypeStruct((nslot, t // LANES, LANES), I32),
                   jax.ShapeDtypeStruct((nslot, t // LANES, LANES), F32)],
        scratch_shapes=[pltpu.VMEM((wqt_bf.shape[0], ROUTE_TOKENS), BF16),
                        pltpu.VMEM((2 * nblk * LAY_PITCH, LANES), F32),
                        pltpu.VMEM((2 * N_HEADS * TOPK, nblk, LANES), F32),
                        pltpu.VMEM((2 * N_HEADS * TOPK, nblk, LANES), I32)],
        compiler_params=_cparams(("parallel",)),
        name="route",
    )(h2_flat, wqt_bf, keys2)
    e1, e2, gate = [o.reshape(nslot, t) for o in outs]
    return e1.T, e2.T, e2, gate.T


GATE_ROWS = LANES // 2
GATE_PITCH = GATE_ROWS + 4
SUBLANES = 8
BUILD_GROUP = 128
HI_MASK = -65536
PEER_CHUNK = 2048
PEER_TOKENS = 512


def _peer_kernel(h2_ref, e1_ref, e2_ref, e2t_ref, gate_ref, ut_ref, v_ref, x1_ref, gt2_ref, gf_ref,
                 y_ref, gmat_scr, w_scr, *, tp, ec):
    c = pl.program_id(1)
    nsub = ec // LANES

    @pl.when(c == 0)
    def _build_gate_matrices():
        y_ref[...] = jnp.zeros_like(y_ref)
        key_id = lax.broadcasted_iota(I32, (LANES, LANES), 0)

        lane_id = lax.broadcasted_iota(I32, (LANES, LANES), 1)

        def body(tb, carry):
            lane0 = pl.multiple_of(tb * BUILD_GROUP, BUILD_GROUP)
            cols = e2t_ref[:, pl.ds(lane0, BUILD_GROUP)]
            for sub in range(BUILD_GROUP // SUBLANES):
                base = pl.multiple_of(lane0 + sub * SUBLANES, SUBLANES)
                r1 = e1_ref[pl.ds(base, SUBLANES), :]
                r2 = e2_ref[pl.ds(base, SUBLANES), :]
                gg = gate_ref[pl.ds(base, SUBLANES), :]
                for i in range(SUBLANES):
                    t = sub * SUBLANES + i
                    a = jnp.where(key_id == r1[i:i + 1, :], gg[i:i + 1, :], 0.0).astype(BF16)
                    if i % 2:
                        b = jnp.where(key_id == r2[i:i + 1, :], 1.0, 0.0).astype(BF16)
                        gm = lax.dot_general(a, b, (((1,), (1,)), ((), ())),
                                             preferred_element_type=F32)
                    else:
                        bt = jnp.where(lane_id == cols[:, t:t + 1], 1.0, 0.0).astype(BF16)
                        gm = jnp.dot(a, bt, preferred_element_type=F32)
                    bits = lax.bitcast_convert_type(gm, I32) + 32768
                    packed = ((bits[:GATE_ROWS] & HI_MASK)
                              | lax.shift_right_logical(bits[GATE_ROWS:], 16))
                    gmat_scr[pl.ds((base + i) * GATE_PITCH, GATE_ROWS), :] = packed
            return carry

        lax.fori_loop(0, tp // BUILD_GROUP, body, 0)

    h2 = h2_ref[...]
    for jb in range(ec // MXU_DEPTH):
        lo = jb * MXU_DEPTH
        s = jnp.dot(h2, ut_ref[0, :, lo:lo + MXU_DEPTH], preferred_element_type=F32)
        for u in range(MXU_DEPTH // LANES):
            k1 = c * nsub + jb * (MXU_DEPTH // LANES) + u
            word = gmat_scr[pl.ds(k1 % GATE_ROWS, tp, stride=GATE_PITCH), :]
            shift = (k1 // GATE_ROWS) * 16
            gi = lax.bitcast_convert_type(lax.shift_left(word, shift) & HI_MASK, F32)
            su = s[:, u * LANES:(u + 1) * LANES]
            w_scr[:, lo + u * LANES:lo + (u + 1) * LANES] = (
                (0.5 * su) * (1.0 + lax.erf(su * INV_SQRT2)) * gi).astype(BF16)
    y_ref[...] += jnp.dot(w_scr[...], v_ref[...], preferred_element_type=F32)

    @pl.when(c == pl.num_programs(1) - 1)
    def _finish():
        xo = x1_ref[...] + gt2_ref[0] * y_ref[...]
        y_ref[...] = _rms(xo) * gf_ref[...]


def _peer(h2_flat, e1, e2, e2t, gate, ut_bf, v_bf, x1_flat, gt2, g_final, tokens_per_batch, tp):
    t, d = h2_flat.shape
    ec = ut_bf.shape[2]
    nc = v_bf.shape[0] // ec
    nslot = e1.shape[1]
    tiles_per_batch = tokens_per_batch // tp
    tok = lambda i, c: (i, 0)
    once = pl.Buffered(1)
    return pl.pallas_call(
        functools.partial(_peer_kernel, tp=tp, ec=ec),
        grid=(t // tp, nc),
        in_specs=[pl.BlockSpec((tp, d), tok),
                  pl.BlockSpec((tp, nslot), tok),
                  pl.BlockSpec((tp, nslot), tok),
                  pl.BlockSpec((nslot, tp), lambda i, c: (0, i)),
                  pl.BlockSpec((tp, nslot), tok),
                  pl.BlockSpec((1, d, ec), lambda i, c: (c, 0, 0)),
                  pl.BlockSpec((ec, d), lambda i, c: (c, 0)),
                  pl.BlockSpec((tp, d), tok, pipeline_mode=once),
                  pl.BlockSpec((1, 1, d), lambda i, c: (i // tiles_per_batch, 0, 0)),
                  pl.BlockSpec((1, d), lambda i, c: (0, 0))],
        out_specs=pl.BlockSpec((tp, d), tok),
        out_shape=jax.ShapeDtypeStruct((t, d), F32),
        scratch_shapes=[pltpu.VMEM((tp * GATE_PITCH, LANES), I32),
                        pltpu.VMEM((tp, ec), BF16)],
        compiler_params=_cparams(("parallel", "arbitrary")),
        name="peer",
    )(h2_flat, e1, e2, e2t, gate, ut_bf, v_bf, x1_flat, gt2, g_final)


def _rope_tables(s):
    inv_freq = 1.0 / (ROPE_THETA ** (jnp.arange(0, HEAD_DIM, 2, dtype=F32) / HEAD_DIM))
    ang = jnp.arange(s, dtype=F32)[:, None] * inv_freq[None, :]
    ang = jnp.concatenate([ang, ang, ang, ang], axis=-1)
    cos, sin = jnp.cos(ang), jnp.sin(ang)
    low = (jnp.arange(LANES) % HEAD_DIM) < (HEAD_DIM // 2)
    return cos, jnp.where(low, -sin, 0.0), jnp.where(low, 0.0, sin)


def _pick_tile(n, want):
    t = min(n, want)
    while n % t:
        t //= 2
    return t


def _trunk(x, mod, wts, lam_init):
    b, s, d = x.shape
    sh1, sc1, gt1, sh2, sc2, gt2 = [m.reshape(b, 1, d) for m in jnp.split(mod, 6, axis=-1)]
    cos, sa, sb = _rope_tables(s)
    p, q, k, v = _front(x, sc1, sh1, wts["g_norm1"], wts["w_in"], cos, sa, sb, _pick_tile(s, 512))
    mixed = _pool(p, wts["pool_w"], wts["pool_scale"])
    att = _attention(q, k, v, wts["lamv"], wts["g_sub"], lam_init,
                     _pick_tile(s, ATTN_SCORE_BYTES // (2 * 4 * s)),
                     _pick_tile(s, 512))
    x1, h2 = _merge(x, mixed, att, (sc1, sh1, gt1, sc2, sh2), wts["g_norm1"], wts["g_norm2"],
                    wts["w_pool_out"], wts["w_att_out"], wts["w_gate"], wts["b_gate"],
                    wts["w_out"], _pick_tile(s, 512))
    t = b * s
    h2f = h2.reshape(t, d)
    e1, e2, e2t, gate = _route(h2f, wts["w_peer_qt"], wts["peer_keys"])
    y = _peer(h2f, e1, e2, e2t, gate, wts["peer_ut"], wts["peer_v"], x1.reshape(t, d),
              gt2, wts["g_final"], s, _pick_tile(s, PEER_TOKENS))
    return y.reshape(b, s, d)


def kernel(x_prompt, x_sample, c_prompt, c_sample, w_ada, b_ada, g_norm1, w_in, pool_w, pool_scale,
           w_pool_out, lam_q1, lam_k1, lam_q2, lam_k2, g_sub, w_att_out, w_gate, b_gate, w_out,
           g_norm2, w_peer_q, peer_keys, peer_u, peer_v, g_final):
    depth = w_ada.shape[0]
    assert depth == 1, "single-layer trunk"
    l = 0
    lam_init = 0.8 - 0.6 * math.exp(-0.3 * l)
    d = x_prompt.shape[-1]
    wts = {
        "g_norm1": g_norm1[l][None, :], "g_norm2": g_norm2[l][None, :], "g_final": g_final[None, :],
        "w_in": w_in[l].astype(BF16),
        "pool_w": pool_w[l].astype(BF16), "pool_scale": pool_scale[l][None, :],
        "w_pool_out": w_pool_out[l].astype(BF16),
        "lamv": jnp.stack([lam_q1[l], lam_k1[l], lam_q2[l], lam_k2[l]]).astype(F32),
        "g_sub": g_sub[l][None, :],
        "w_att_out": w_att_out[l].astype(BF16),
        "w_gate": w_gate[l].astype(BF16), "b_gate": b_gate[l][None, :],
        "w_out": w_out[l].astype(BF16),
        "w_peer_qt": w_peer_q[l].T.astype(BF16),
        "peer_keys": peer_keys[l].astype(BF16),
        "peer_ut": peer_u[l].astype(BF16).reshape(-1, PEER_CHUNK, d).transpose(0, 2, 1),
        "peer_v": peer_v[l].astype(BF16),
    }
    nb = x_prompt.shape[0]
    mod = _ada(jnp.concatenate([c_prompt, c_sample], axis=0), w_ada[l].astype(BF16), b_ada[l][None, :])
    y_prompt = _trunk(x_prompt, mod[:nb], wts, lam_init)
    y_sample = _trunk(x_sample, mod[nb:], wts, lam_init)
    return (y_prompt, y_sample)
```

```python
import functools
import math

import jax
import jax.numpy as jnp
from jax import lax
from jax.experimental import pallas as pl
from jax.experimental.pallas import tpu as pltpu

F32 = jnp.float32
BF16 = jnp.bfloat16
I32 = jnp.int32

NORM_EPS = 1e-6
ROPE_THETA = 10000.0
POOL_WINDOWS = (2, 4, 8, 16)
POOL_PAD = 16
N_HEADS = 8
HEAD_DIM = 64
LANES = 128
MXU_DEPTH = 256
TOPK = 16
CAND_ROWS = tuple((a, TOPK // (a + 1)) for a in range(TOPK))
VMEM_LIMIT = 56 * 1024 * 1024
ATTN_SCORE_BYTES = 16 * 1024 * 1024
INV_SQRT2 = 1.0 / math.sqrt(2.0)
LOG2E = 1.0 / math.log(2.0)
NEG_INF = float("-inf")


def _cparams(sem):
    return pltpu.CompilerParams(dimension_semantics=sem, vmem_limit_bytes=VMEM_LIMIT)


def _rms(x):
    return x * lax.rsqrt(jnp.mean(x * x, axis=-1, keepdims=True) + NORM_EPS)


def _ada_kernel(c_ref, w_ref, b_ref, o_ref):
    c = c_ref[...]
    a = (c * jax.nn.sigmoid(c)).astype(BF16)
    o_ref[...] = jnp.dot(a, w_ref[...], preferred_element_type=F32) + b_ref[...]


def _ada(c, w_bf, b):
    n, d = c.shape
    nout = w_bf.shape[1]
    return pl.pallas_call(
        _ada_kernel,
        grid=(nout // d,),
        in_specs=[pl.BlockSpec((n, d), lambda j: (0, 0)),
                  pl.BlockSpec((d, d), lambda j: (0, j)),
                  pl.BlockSpec((1, d), lambda j: (0, j))],
        out_specs=pl.BlockSpec((n, d), lambda j: (0, j)),
        out_shape=jax.ShapeDtypeStruct((n, nout), F32),
        compiler_params=_cparams(("parallel",)),
        name="ada",
    )(c, w_bf, b)


def _front_kernel(x_ref, sc_ref, sh_ref, g_ref, w_ref, cos_ref, sa_ref, sb_ref,
                  p_ref, q_ref, k_ref, v_ref, *, d, pool_dim):
    x = x_ref[0]
    h = (_rms(x) * g_ref[...]) * (1.0 + sc_ref[0]) + sh_ref[0]
    proj = jnp.dot(h.astype(BF16), w_ref[...], preferred_element_type=F32)
    p_ref[0] = proj[:, :pool_dim]
    cos, sa, sb = cos_ref[...], sa_ref[...], sb_ref[...]
    half = HEAD_DIM // 2

    def rope(t):
        return (t * cos + pltpu.roll(t, LANES - half, axis=1) * sa
                + pltpu.roll(t, half, axis=1) * sb)

    qscale = HEAD_DIM ** -0.5 * LOG2E
    for j in range(d // LANES):
        o = pool_dim + j * LANES
        q_ref[0, j] = (rope(proj[:, o:o + LANES]) * qscale).astype(BF16)
        o = pool_dim + d + j * LANES
        k_ref[0, j] = rope(proj[:, o:o + LANES]).astype(BF16)
        o = pool_dim + 2 * d + j * LANES
        v_ref[0, j] = proj[:, o:o + LANES].astype(BF16)


def _front(x, sc, sh, g, w_in_bf, cos, sa, sb, tm):
    b, s, d = x.shape
    in_dim = w_in_bf.shape[1]
    pool_dim = in_dim - 3 * d
    nh = d // LANES
    row = lambda i, j: (i, j, 0)
    per_b = lambda i, j: (i, 0, 0)
    const = lambda i, j: (0, 0)
    return pl.pallas_call(
        functools.partial(_front_kernel, d=d, pool_dim=pool_dim),
        grid=(b, s // tm),
        in_specs=[pl.BlockSpec((1, tm, d), row),
                  pl.BlockSpec((1, 1, d), per_b),
                  pl.BlockSpec((1, 1, d), per_b),
                  pl.BlockSpec((1, d), const),
                  pl.BlockSpec((d, in_dim), const),
                  pl.BlockSpec((tm, LANES), lambda i, j: (j, 0)),
                  pl.BlockSpec((tm, LANES), lambda i, j: (j, 0)),
                  pl.BlockSpec((tm, LANES), lambda i, j: (j, 0))],
        out_specs=[pl.BlockSpec((1, tm, pool_dim), row)]
                  + [pl.BlockSpec((1, nh, tm, LANES), lambda i, j: (i, 0, j, 0))] * 3,
        out_shape=[jax.ShapeDtypeStruct((b, s, pool_dim), F32)]
                  + [jax.ShapeDtypeStruct((b, nh, s, LANES), BF16)] * 3,
        compiler_params=_cparams(("parallel", "parallel")),
        name="front",
    )(x, sc, sh, g, w_in_bf, cos, sa, sb)


def _pool_kernel(p_ref, w_ref, scale_ref, o_ref, pad_ref, *, s):
    pos = lax.broadcasted_iota(I32, (s, LANES), 0)
    zeros = jnp.zeros((POOL_PAD, LANES), F32)
    for g, win in enumerate(POOL_WINDOWS):
        half = win // 2
        xg = p_ref[0, :, g * LANES:(g + 1) * LANES]
        pad_ref[0:POOL_PAD, :] = zeros
        pad_ref[POOL_PAD:POOL_PAD + s, :] = xg
        pad_ref[POOL_PAD + s:2 * POOL_PAD + s, :] = zeros
        wsum = pad_ref[POOL_PAD - half:POOL_PAD - half + s, :]
        for dlt in range(-half + 1, half):
            wsum = wsum + pad_ref[POOL_PAD + dlt:POOL_PAD + dlt + s, :]
        cnt = (jnp.minimum(pos + half, s) - jnp.maximum(pos - half, 0)).astype(F32)
        diff = wsum / cnt - xg
        mixed = jnp.dot(diff.astype(BF16), w_ref[g], preferred_element_type=F32)
        o_ref[0, :, g * LANES:(g + 1) * LANES] = (
            mixed * scale_ref[:, g * LANES:(g + 1) * LANES]).astype(BF16)


def _pool(p, pool_w_bf, pool_scale):
    b, s, pool_dim = p.shape
    ng = pool_w_bf.shape[0]
    return pl.pallas_call(
        functools.partial(_pool_kernel, s=s),
        grid=(b,),
        in_specs=[pl.BlockSpec((1, s, pool_dim), lambda i: (i, 0, 0)),
                  pl.BlockSpec((ng, LANES, LANES), lambda i: (0, 0, 0)),
                  pl.BlockSpec((1, pool_dim), lambda i: (0, 0))],
        out_specs=pl.BlockSpec((1, s, pool_dim), lambda i: (i, 0, 0)),
        out_shape=jax.ShapeDtypeStruct((b, s, pool_dim), BF16),
        scratch_shapes=[pltpu.VMEM((s + 2 * POOL_PAD, LANES), F32)],
        compiler_params=_cparams(("parallel",)),
        name="pool",
    )(p, pool_w_bf, pool_scale)


def _attn_kernel(lamv_ref, q_ref, k_ref, v_ref, gsub_ref, o_ref, s_scr, vext_scr,
                 *, lam_init, s, tq, tk):
    lv = lamv_ref[...]
    lam = (jnp.exp(jnp.sum(lv[0:1] * lv[1:2], axis=-1, keepdims=True))
           - jnp.exp(jnp.sum(lv[2:3] * lv[3:4], axis=-1, keepdims=True)) + lam_init)
    @pl.when(pl.program_id(2) == 0)
    def _extend_values():
        vext_scr[:, :LANES] = v_ref[0, 0]
        vext_scr[:, LANES:] = jnp.ones((s, LANES), BF16)

    q = q_ref[0, 0]
    first = lax.broadcasted_iota(I32, (1, LANES), 1) < HEAD_DIM
    zero = jnp.zeros_like(q)
    qs = (jnp.where(first, q, zero), jnp.where(first, zero, q))
    nt = (((1,), (1,)), ((), ()))
    nchunk = s // tk
    mx = [jnp.full((tq, LANES), NEG_INF, F32), jnp.full((tq, LANES), NEG_INF, F32)]
    for c in range(nchunk):
        kc = k_ref[0, 0, c * tk:(c + 1) * tk, :]
        for u in range(2):
            sc = lax.dot_general(qs[u], kc, nt, preferred_element_type=F32)
            s_scr[u, :, c * tk:(c + 1) * tk] = sc
            for jj in range(tk // LANES):
                mx[u] = jnp.maximum(mx[u], sc[:, jj * LANES:(jj + 1) * LANES])
    mx = [jnp.max(m, axis=-1, keepdims=True) for m in mx]
    acc = [jnp.zeros((tq, 2 * LANES), F32), jnp.zeros((tq, 2 * LANES), F32)]
    for c in range(nchunk):
        vc = vext_scr[c * tk:(c + 1) * tk, :]
        for u in range(2):
            p = jnp.exp2(s_scr[u, :, c * tk:(c + 1) * tk] - mx[u]).astype(BF16)
            acc[u] = acc[u] + jnp.dot(p, vc, preferred_element_type=F32)
    c1 = 1.0 / acc[0][:, LANES:LANES + 1]
    c2 = lam / acc[1][:, LANES:LANES + 1]
    o = acc[0][:, :LANES] * c1 - acc[1][:, :LANES] * c2
    o_ref[0, 0] = ((_rms(o) * gsub_ref[...]) * (1.0 - lam_init)).astype(BF16)


def _attention(q, k, v, lamv, g_sub, lam_init, tq, tk):
    b, nh, s, _ = q.shape
    return pl.pallas_call(
        functools.partial(_attn_kernel, lam_init=lam_init, s=s, tq=tq, tk=tk),
        grid=(b, nh, s // tq),
        in_specs=[pl.BlockSpec((4, HEAD_DIM), lambda i, h, j: (0, 0)),
                  pl.BlockSpec((1, 1, tq, LANES), lambda i, h, j: (i, h, j, 0)),
                  pl.BlockSpec((1, 1, s, LANES), lambda i, h, j: (i, h, 0, 0)),
                  pl.BlockSpec((1, 1, s, LANES), lambda i, h, j: (i, h, 0, 0)),
                  pl.BlockSpec((1, LANES), lambda i, h, j: (0, 0))],
        out_specs=pl.BlockSpec((1, 1, tq, LANES), lambda i, h, j: (i, h, j, 0)),
        out_shape=jax.ShapeDtypeStruct((b, nh, s, LANES), BF16),
        scratch_shapes=[pltpu.VMEM((2, tq, s), F32), pltpu.VMEM((s, 2 * LANES), BF16)],
        compiler_params=_cparams(("parallel", "parallel", "arbitrary")),
        name="attn",
    )(lamv, q, k, v, g_sub)


def _merge_kernel(x_ref, mixed_ref, att_ref, sc1_ref, sh1_ref, gt1_ref, sc2_ref, sh2_ref,
                  g1_ref, g2_ref, wpool_ref, watt_ref, wgate_ref, bgate_ref, wout_ref,
                  x1_ref, h2_ref, *, d):
    x = x_ref[0]
    h = ((_rms(x) * g1_ref[...]) * (1.0 + sc1_ref[0]) + sh1_ref[0]).astype(BF16)
    gates = jax.nn.sigmoid(jnp.dot(h, wgate_ref[...], preferred_element_type=F32) + bgate_ref[...])
    y_pool = jnp.dot(mixed_ref[0], wpool_ref[...], preferred_element_type=F32)
    att = jnp.concatenate([att_ref[0, hd] for hd in range(d // LANES)], axis=1)
    y_att = jnp.dot(att, watt_ref[...], preferred_element_type=F32)
    merged = gates[:, :d] * y_pool + gates[:, d:] * y_att
    x1 = x + gt1_ref[0] * jnp.dot(merged.astype(BF16), wout_ref[...], preferred_element_type=F32)
    x1_ref[0] = x1
    h2 = (_rms(x1) * g2_ref[...]) * (1.0 + sc2_ref[0]) + sh2_ref[0]
    h2_ref[0] = h2.astype(BF16)


def _merge(x, mixed, att, mods, g1, g2, wpool, watt, wgate, bgate, wout, tm):
    b, s, d = x.shape
    pool_dim = mixed.shape[-1]
    row = lambda i, j: (i, j, 0)
    per_b = lambda i, j: (i, 0, 0)
    const = lambda i, j: (0, 0)
    sc1, sh1, gt1, sc2, sh2 = mods
    return pl.pallas_call(
        functools.partial(_merge_kernel, d=d),
        grid=(b, s // tm),
        in_specs=[pl.BlockSpec((1, tm, d), row),
                  pl.BlockSpec((1, tm, pool_dim), row),
                  pl.BlockSpec((1, d // LANES, tm, LANES), lambda i, j: (i, 0, j, 0))]
                 + [pl.BlockSpec((1, 1, d), per_b)] * 5
                 + [pl.BlockSpec((1, d), const), pl.BlockSpec((1, d), const),
                    pl.BlockSpec((pool_dim, d), const), pl.BlockSpec((d, d), const),
                    pl.BlockSpec((d, 2 * d), const), pl.BlockSpec((1, 2 * d), const),
                    pl.BlockSpec((d, d), const)],
        out_specs=[pl.BlockSpec((1, tm, d), row), pl.BlockSpec((1, tm, d), row)],
        out_shape=[jax.ShapeDtypeStruct((b, s, d), F32), jax.ShapeDtypeStruct((b, s, d), BF16)],
        compiler_params=_cparams(("parallel", "parallel")),
        name="merge",
    )(x, mixed, att, sc1, sh1, gt1, sc2, sh2, g1, g2, wpool, watt, wgate, bgate, wout)


ROUTE_TOKENS = 8 * LANES
LAY_PITCH = LANES + 8


def _sort16_network():
    n, pairs, p = TOPK, [], 1
    while p < n:
        k = p
        while k >= 1:
            for j in range(k % p, n - k, 2 * k):
                for i in range(min(k, n - j - k)):
                    if (i + j) // (2 * p) == (i + j + k) // (2 * p):
                        pairs.append((i + j, i + j + k))
            k //= 2
        p *= 2
    return tuple(pairs)


SORT16 = _sort16_network()


def _swap_mask(a, b):
    (av, ai), (bv, bi) = a, b
    return (bv > av) | ((bv == av) & (bi < ai))


def _order(a, b):
    sw = _swap_mask(a, b)
    first = (jnp.maximum(a[0], b[0]), jnp.where(sw, b[1], a[1]))
    second = (jnp.minimum(a[0], b[0]), jnp.where(sw, a[1], b[1]))
    return first, second


def _first_of(a, b):
    return jnp.maximum(a[0], b[0]), jnp.where(_swap_mask(a, b), b[1], a[1])


def _sort16(items):
    items = list(items)
    for i, j in SORT16:
        items[i], items[j] = _order(items[i], items[j])
    return items


def _merge_top16(a, b):
    c = [_first_of(a[i], b[TOPK - 1 - i]) for i in range(TOPK)]
    stride = TOPK // 2
    while stride:
        for i in range(TOPK):
            if not i & stride:
                c[i], c[i + stride] = _order(c[i], c[i + stride])
        stride //= 2
    return c


def _route_kernel(h2_ref, wqt_ref, keys_ref, e1_ref, e2_ref, gate_ref,
                  qt_scr, lay_scr, tv_scr, ti_scr):
    nblk = ROUTE_TOKENS // LANES
    qt = lax.dot_general(wqt_ref[...], h2_ref[...], (((1,), (1,)), ((), ())),
                         preferred_element_type=F32)
    qt_scr[...] = qt.astype(BF16)

    def set_body(h, carry):
        for part in range(2):
            si = 2 * h + part
            r0 = pl.multiple_of(si * HEAD_DIM, HEAD_DIM)
            st = jnp.dot(keys_ref[si], qt_scr[pl.ds(r0, HEAD_DIM), :],
                         preferred_element_type=F32)
            for c in range(nblk):
                row = (part * nblk + c) * LAY_PITCH
                lay_scr[row:row + LANES, :] = st[:, c * LANES:(c + 1) * LANES]
        lists = []
        for g in range(LANES // TOPK):
            items = []
            for k in range(g * TOPK, (g + 1) * TOPK):
                v = jnp.stack([lay_scr[pl.ds(part * nblk * LAY_PITCH + k, nblk, stride=LAY_PITCH), :]
                               for part in range(2)])
                items.append((v, jnp.full((2, nblk, LANES), k, I32)))
            lists.append(_sort16(items))
        while len(lists) > 1:
            lists = [_merge_top16(lists[2 * i], lists[2 * i + 1]) for i in range(len(lists) // 2)]
        for a, (v, idx) in enumerate(lists[0]):
            for part in range(2):
                tv_scr[(2 * h + part) * TOPK + a] = v[part]
                ti_scr[(2 * h + part) * TOPK + a] = idx[part]
        return carry

    lax.fori_loop(0, N_HEADS, set_body, 0)

    def head_body(h, carry):
        b1 = 2 * h * TOPK
        b2 = b1 + TOPK
        cv, ce = [], []
        for a, nb in CAND_ROWS:
            va = tv_scr[b1 + a]
            ea = ti_scr[b1 + a] * LANES
            for b in range(nb):
                cv.append(va + tv_scr[b2 + b])
                ce.append(ea + ti_scr[b2 + b])
        best_s, best_e = [], []
        for _ in range(TOPK):
            nodes = list(zip(cv, ce))
            while len(nodes) > 1:
                nxt = []
                for i in range(0, len(nodes) - 1, 2):
                    (lv, le), (rv, re) = nodes[i], nodes[i + 1]
                    nxt.append((jnp.maximum(lv, rv), jnp.where(rv > lv, re, le)))
                if len(nodes) % 2:
                    nxt.append(nodes[-1])
                nodes = nxt
            m, e = nodes[0]
            best_s.append(m)
            best_e.append(e)
            cv = [jnp.where(c_e == e, NEG_INF, c_v) for c_v, c_e in zip(cv, ce)]
        ex = [jnp.exp(b - best_s[0]) for b in best_s]
        z = ex[0]
        for t in ex[1:]:
            z = z + t
        for k in range(TOPK):
            e1_ref[h * TOPK + k] = lax.shift_right_logical(best_e[k], 7)
            e2_ref[h * TOPK + k] = best_e[k] & (LANES - 1)
            gate_ref[h * TOPK + k] = ex[k] / z
        return carry

    lax.fori_loop(0, N_HEADS, head_body, 0)


def _route(h2_flat, wqt_bf, keys_bf):
    t, d = h2_flat.shape
    assert t % ROUTE_TOKENS == 0
    nslot = N_HEADS * TOPK
    nblk = ROUTE_TOKENS // LANES
    keys2 = keys_bf.reshape(2 * N_HEADS, LANES, HEAD_DIM)
    out = pl.BlockSpec((nslot, nblk, LANES), lambda i: (0, i, 0))
    outs = pl.pallas_call(
        _route_kernel,
        grid=(t // ROUTE_TOKENS,),
        in_specs=[pl.BlockSpec((ROUTE_TOKENS, d), lambda i: (i, 0)),
                  pl.BlockSpec(wqt_bf.shape, lambda i: (0, 0)),
                  pl.BlockSpec(keys2.shape, lambda i: (0, 0, 0))],
        out_specs=[out, out, out],
        out_shape=[jax.ShapeDtypeStruct((nslot, t // LANES, LANES), I32),
                   jax.ShapeDtypeStruct((nslot, t // LANES, LANES), I32),
                   jax.ShapeDtypeStruct((nslot, t // LANES, LANES), F32)],
        scratch_shapes=[pltpu.VMEM((wqt_bf.shape[0], ROUTE_TOKENS), BF16),
                        pltpu.VMEM((2 * nblk * LAY_PITCH, LANES), F32),
                        pltpu.VMEM((2 * N_HEADS * TOPK, nblk, LANES), F32),
                        pltpu.VMEM((2 * N_HEADS * TOPK, nblk, LANES), I32)],
        compiler_params=_cparams(("parallel",)),
        name="route",
    )(h2_flat, wqt_bf, keys2)
    e1, e2, gate = [o.reshape(nslot, t) for o in outs]
    return e1.T, e2.T, e2, gate.T


GATE_ROWS = LANES // 2
GATE_PITCH = GATE_ROWS + 4
SUBLANES = 8
BUILD_GROUP = 128
HI_MASK = -65536
PEER_CHUNK = 2048
PEER_TOKENS = 512


def _peer_kernel(h2_ref, e1_ref, e2_ref, e2t_ref, gate_ref, ut_ref, v_ref, x1_ref, gt2_ref, gf_ref,
                 y_ref, gmat_scr, w_scr, *, tp, ec):
    c = pl.program_id(1)
    nsub = ec // LANES

    @pl.when(c == 0)
    def _build_gate_matrices():
        y_ref[...] = jnp.zeros_like(y_ref)
        key_id = lax.broadcasted_iota(I32, (LANES, LANES), 0)

        lane_id = lax.broadcasted_iota(I32, (LANES, LANES), 1)

        def body(tb, carry):
            lane0 = pl.multiple_of(tb * BUILD_GROUP, BUILD_GROUP)
            cols = e2t_ref[:, pl.ds(lane0, BUILD_GROUP)]
            for sub in range(BUILD_GROUP // SUBLANES):
                base = pl.multiple_of(lane0 + sub * SUBLANES, SUBLANES)
                r1 = e1_ref[pl.ds(base, SUBLANES), :]
                r2 = e2_ref[pl.ds(base, SUBLANES), :]
                gg = gate_ref[pl.ds(base, SUBLANES), :]
                for i in range(SUBLANES):
                    t = sub * SUBLANES + i
                    a = jnp.where(key_id == r1[i:i + 1, :], gg[i:i + 1, :], 0.0).astype(BF16)
                    if i % 2:
                        b = jnp.where(key_id == r2[i:i + 1, :], 1.0, 0.0).astype(BF16)
                        gm = lax.dot_general(a, b, (((1,), (1,)), ((), ())),
                                             preferred_element_type=F32)
                    else:
                        bt = jnp.where(lane_id == cols[:, t:t + 1], 1.0, 0.0).astype(BF16)
                        gm = jnp.dot(a, bt, preferred_element_type=F32)
                    gmat_scr[pl.ds((base + i) * GATE_PITCH, GATE_ROWS), :] = pltpu.bitcast(
                        gm.astype(BF16), I32)
            return carry

        lax.fori_loop(0, tp // BUILD_GROUP, body, 0)

    h2 = h2_ref[...]
    for jb in range(ec // MXU_DEPTH):
        lo = jb * MXU_DEPTH
        s = jnp.dot(h2, ut_ref[0, :, lo:lo + MXU_DEPTH], preferred_element_type=F32)
        for u in range(MXU_DEPTH // LANES):
            k1 = c * nsub + jb * (MXU_DEPTH // LANES) + u
            word = gmat_scr[pl.ds(k1 // 2, tp, stride=GATE_PITCH), :]
            shift = (1 - k1 % 2) * 16
            gi = lax.bitcast_convert_type(lax.shift_left(word, shift) & HI_MASK, F32)
            su = s[:, u * LANES:(u + 1) * LANES]
            w_scr[:, lo + u * LANES:lo + (u + 1) * LANES] = (
                (0.5 * su) * (1.0 + lax.erf(su * INV_SQRT2)) * gi).astype(BF16)
    y_ref[...] += jnp.dot(w_scr[...], v_ref[...], preferred_element_type=F32)

    @pl.when(c == pl.num_programs(1) - 1)
    def _finish():
        xo = x1_ref[...] + gt2_ref[0] * y_ref[...]
        y_ref[...] = _rms(xo) * gf_ref[...]


def _peer(h2_flat, e1, e2, e2t, gate, ut_bf, v_bf, x1_flat, gt2, g_final, tokens_per_batch, tp):
    t, d = h2_flat.shape
    ec = ut_bf.shape[2]
    nc = v_bf.shape[0] // ec
    nslot = e1.shape[1]
    tiles_per_batch = tokens_per_batch // tp
    tok = lambda i, c: (i, 0)
    once = pl.Buffered(1)
    return pl.pallas_call(
        functools.partial(_peer_kernel, tp=tp, ec=ec),
        grid=(t // tp, nc),
        in_specs=[pl.BlockSpec((tp, d), tok),
                  pl.BlockSpec((tp, nslot), tok),
                  pl.BlockSpec((tp, nslot), tok),
                  pl.BlockSpec((nslot, tp), lambda i, c: (0, i)),
                  pl.BlockSpec((tp, nslot), tok),
                  pl.BlockSpec((1, d, ec), lambda i, c: (c, 0, 0)),
                  pl.BlockSpec((ec, d), lambda i, c: (c, 0)),
                  pl.BlockSpec((tp, d), tok, pipeline_mode=once),
                  pl.BlockSpec((1, 1, d), lambda i, c: (i // tiles_per_batch, 0, 0)),
                  pl.BlockSpec((1, d), lambda i, c: (0, 0))],
        out_specs=pl.BlockSpec((tp, d), tok),
        out_shape=jax.ShapeDtypeStruct((t, d), F32),
        scratch_shapes=[pltpu.VMEM((tp * GATE_PITCH, LANES), I32),
                        pltpu.VMEM((tp, ec), BF16)],
        compiler_params=_cparams(("parallel", "arbitrary")),
        name="peer",
    )(h2_flat, e1, e2, e2t, gate, ut_bf, v_bf, x1_flat, gt2, g_final)


def _rope_tables(s):
    inv_freq = 1.0 / (ROPE_THETA ** (jnp.arange(0, HEAD_DIM, 2, dtype=F32) / HEAD_DIM))
    ang = jnp.arange(s, dtype=F32)[:, None] * inv_freq[None, :]
    ang = jnp.concatenate([ang, ang, ang, ang], axis=-1)
    cos, sin = jnp.cos(ang), jnp.sin(ang)
    low = (jnp.arange(LANES) % HEAD_DIM) < (HEAD_DIM // 2)
    return cos, jnp.where(low, -sin, 0.0), jnp.where(low, 0.0, sin)


def _pick_tile(n, want):
    t = min(n, want)
    while n % t:
        t //= 2
    return t


def _trunk(x, mod, wts, lam_init):
    b, s, d = x.shape
    sh1, sc1, gt1, sh2, sc2, gt2 = [m.reshape(b, 1, d) for m in jnp.split(mod, 6, axis=-1)]
    cos, sa, sb = _rope_tables(s)
    p, q, k, v = _front(x, sc1, sh1, wts["g_norm1"], wts["w_in"], cos, sa, sb, _pick_tile(s, 512))
    mixed = _pool(p, wts["pool_w"], wts["pool_scale"])
    att = _attention(q, k, v, wts["lamv"], wts["g_sub"], lam_init,
                     _pick_tile(s, ATTN_SCORE_BYTES // (2 * 4 * s)),
                     _pick_tile(s, 512))
    x1, h2 = _merge(x, mixed, att, (sc1, sh1, gt1, sc2, sh2), wts["g_norm1"], wts["g_norm2"],
                    wts["w_pool_out"], wts["w_att_out"], wts["w_gate"], wts["b_gate"],
                    wts["w_out"], _pick_tile(s, 512))
    t = b * s
    h2f = h2.reshape(t, d)
    e1, e2, e2t, gate = _route(h2f, wts["w_peer_qt"], wts["peer_keys"])
    y = _peer(h2f, e1, e2, e2t, gate, wts["peer_ut"], wts["peer_v"], x1.reshape(t, d),
              gt2, wts["g_final"], s, _pick_tile(s, PEER_TOKENS))
    return y.reshape(b, s, d)


def kernel(x_prompt, x_sample, c_prompt, c_sample, w_ada, b_ada, g_norm1, w_in, pool_w, pool_scale,
           w_pool_out, lam_q1, lam_k1, lam_q2, lam_k2, g_sub, w_att_out, w_gate, b_gate, w_out,
           g_norm2, w_peer_q, peer_keys, peer_u, peer_v, g_final):
    depth = w_ada.shape[0]
    assert depth == 1, "single-layer trunk"
    l = 0
    lam_init = 0.8 - 0.6 * math.exp(-0.3 * l)
    d = x_prompt.shape[-1]
    wts = {
        "g_norm1": g_norm1[l][None, :], "g_norm2": g_norm2[l][None, :], "g_final": g_final[None, :],
        "w_in": w_in[l].astype(BF16),
        "pool_w": pool_w[l].astype(BF16), "pool_scale": pool_scale[l][None, :],
        "w_pool_out": w_pool_out[l].astype(BF16),
        "lamv": jnp.stack([lam_q1[l], lam_k1[l], lam_q2[l], lam_k2[l]]).astype(F32),
        "g_sub": g_sub[l][None, :],
        "w_att_out": w_att_out[l].astype(BF16),
        "w_gate": w_gate[l].astype(BF16), "b_gate": b_gate[l][None, :],
        "w_out": w_out[l].astype(BF16),
        "w_peer_qt": w_peer_q[l].T.astype(BF16),
        "peer_keys": peer_keys[l].astype(BF16),
        "peer_ut": peer_u[l].astype(BF16).reshape(-1, PEER_CHUNK, d).transpose(0, 2, 1),
        "peer_v": peer_v[l].astype(BF16),
    }
    nb = x_prompt.shape[0]
    mod = _ada(jnp.concatenate([c_prompt, c_sample], axis=0), w_ada[l].astype(BF16), b_ada[l][None, :])
    y_prompt = _trunk(x_prompt, mod[:nb], wts, lam_init)
    y_sample = _trunk(x_sample, mod[nb:], wts, lam_init)
    return (y_prompt, y_sample)
```

```python
import functools
import math

import jax
import jax.numpy as jnp
from jax import lax
from jax.experimental import pallas as pl
from jax.experimental.pallas import tpu as pltpu

F32 = jnp.float32
BF16 = jnp.bfloat16
I32 = jnp.int32

NORM_EPS = 1e-6
ROPE_THETA = 10000.0
POOL_WINDOWS = (2, 4, 8, 16)
POOL_PAD = 16
N_HEADS = 8
HEAD_DIM = 64
LANES = 128
MXU_DEPTH = 256
TOPK = 16
CAND_ROWS = tuple((a, TOPK // (a + 1)) for a in range(TOPK))
VMEM_LIMIT = 56 * 1024 * 1024
ATTN_SCORE_BYTES = 16 * 1024 * 1024
INV_SQRT2 = 1.0 / math.sqrt(2.0)
LOG2E = 1.0 / math.log(2.0)
NEG_INF = float("-inf")


def _cparams(sem):
    return pltpu.CompilerParams(dimension_semantics=sem, vmem_limit_bytes=VMEM_LIMIT)


def _rms(x):
    return x * lax.rsqrt(jnp.mean(x * x, axis=-1, keepdims=True) + NORM_EPS)


def _ada_kernel(c_ref, w_ref, b_ref, o_ref):
    c = c_ref[...]
    a = (c * jax.nn.sigmoid(c)).astype(BF16)
    o_ref[...] = jnp.dot(a, w_ref[...], preferred_element_type=F32) + b_ref[...]


def _ada(c, w_bf, b):
    n, d = c.shape
    nout = w_bf.shape[1]
    return pl.pallas_call(
        _ada_kernel,
        grid=(nout // d,),
        in_specs=[pl.BlockSpec((n, d), lambda j: (0, 0)),
                  pl.BlockSpec((d, d), lambda j: (0, j)),
                  pl.BlockSpec((1, d), lambda j: (0, j))],
        out_specs=pl.BlockSpec((n, d), lambda j: (0, j)),
        out_shape=jax.ShapeDtypeStruct((n, nout), F32),
        compiler_params=_cparams(("parallel",)),
        name="ada",
    )(c, w_bf, b)


def _front_kernel(x_ref, sc_ref, sh_ref, g_ref, w_ref, cos_ref, sa_ref, sb_ref,
                  p_ref, q_ref, k_ref, v_ref, *, d, pool_dim):
    x = x_ref[0]
    h = (_rms(x) * g_ref[...]) * (1.0 + sc_ref[0]) + sh_ref[0]
    proj = jnp.dot(h.astype(BF16), w_ref[...], preferred_element_type=F32)
    p_ref[0] = proj[:, :pool_dim]
    cos, sa, sb = cos_ref[...], sa_ref[...], sb_ref[...]
    half = HEAD_DIM // 2

    def rope(t):
        return (t * cos + pltpu.roll(t, LANES - half, axis=1) * sa
                + pltpu.roll(t, half, axis=1) * sb)

    qscale = HEAD_DIM ** -0.5 * LOG2E
    for j in range(d // LANES):
        o = pool_dim + j * LANES
        q_ref[0, j] = (rope(proj[:, o:o + LANES]) * qscale).astype(BF16)
        o = pool_dim + d + j * LANES
        k_ref[0, j] = rope(proj[:, o:o + LANES]).astype(BF16)
        o = pool_dim + 2 * d + j * LANES
        v_ref[0, j] = proj[:, o:o + LANES].astype(BF16)


def _front(x, sc, sh, g, w_in_bf, cos, sa, sb, tm):
    b, s, d = x.shape
    in_dim = w_in_bf.shape[1]
    pool_dim = in_dim - 3 * d
    nh = d // LANES
    row = lambda i, j: (i, j, 0)
    per_b = lambda i, j: (i, 0, 0)
    const = lambda i, j: (0, 0)
    return pl.pallas_call(
        functools.partial(_front_kernel, d=d, pool_dim=pool_dim),
        grid=(b, s // tm),
        in_specs=[pl.BlockSpec((1, tm, d), row),
                  pl.BlockSpec((1, 1, d), per_b),
                  pl.BlockSpec((1, 1, d), per_b),
                  pl.BlockSpec((1, d), const),
                  pl.BlockSpec((d, in_dim), const),
                  pl.BlockSpec((tm, LANES), lambda i, j: (j, 0)),
                  pl.BlockSpec((tm, LANES), lambda i, j: (j, 0)),
                  pl.BlockSpec((tm, LANES), lambda i, j: (j, 0))],
        out_specs=[pl.BlockSpec((1, tm, pool_dim), row)]
                  + [pl.BlockSpec((1, nh, tm, LANES), lambda i, j: (i, 0, j, 0))] * 3,
        out_shape=[jax.ShapeDtypeStruct((b, s, pool_dim), F32)]
                  + [jax.ShapeDtypeStruct((b, nh, s, LANES), BF16)] * 3,
        compiler_params=_cparams(("parallel", "parallel")),
        name="front",
    )(x, sc, sh, g, w_in_bf, cos, sa, sb)


def _pool_kernel(p_ref, w_ref, scale_ref, o_ref, pad_ref, *, s):
    pos = lax.broadcasted_iota(I32, (s, LANES), 0)
    zeros = jnp.zeros((POOL_PAD, LANES), F32)
    for g, win in enumerate(POOL_WINDOWS):
        half = win // 2
        xg = p_ref[0, :, g * LANES:(g + 1) * LANES]
        pad_ref[0:POOL_PAD, :] = zeros
        pad_ref[POOL_PAD:POOL_PAD + s, :] = xg
        pad_ref[POOL_PAD + s:2 * POOL_PAD + s, :] = zeros
        wsum = pad_ref[POOL_PAD - half:POOL_PAD - half + s, :]
        for dlt in range(-half + 1, half):
            wsum = wsum + pad_ref[POOL_PAD + dlt:POOL_PAD + dlt + s, :]
        cnt = (jnp.minimum(pos + half, s) - jnp.maximum(pos - half, 0)).astype(F32)
        diff = wsum / cnt - xg
        mixed = jnp.dot(diff.astype(BF16), w_ref[g], preferred_element_type=F32)
        o_ref[0, :, g * LANES:(g + 1) * LANES] = (
            mixed * scale_ref[:, g * LANES:(g + 1) * LANES]).astype(BF16)


def _pool(p, pool_w_bf, pool_scale):
    b, s, pool_dim = p.shape
    ng = pool_w_bf.shape[0]
    return pl.pallas_call(
        functools.partial(_pool_kernel, s=s),
        grid=(b,),
        in_specs=[pl.BlockSpec((1, s, pool_dim), lambda i: (i, 0, 0)),
                  pl.BlockSpec((ng, LANES, LANES), lambda i: (0, 0, 0)),
                  pl.BlockSpec((1, pool_dim), lambda i: (0, 0))],
        out_specs=pl.BlockSpec((1, s, pool_dim), lambda i: (i, 0, 0)),
        out_shape=jax.ShapeDtypeStruct((b, s, pool_dim), BF16),
        scratch_shapes=[pltpu.VMEM((s + 2 * POOL_PAD, LANES), F32)],
        compiler_params=_cparams(("parallel",)),
        name="pool",
    )(p, pool_w_bf, pool_scale)


def _attn_kernel(lamv_ref, q_ref, k_ref, v_ref, gsub_ref, o_ref, s_scr, vext_scr,
                 *, lam_init, s, tq, tk):
    lv = lamv_ref[...]
    lam = (jnp.exp(jnp.sum(lv[0:1] * lv[1:2], axis=-1, keepdims=True))
           - jnp.exp(jnp.sum(lv[2:3] * lv[3:4], axis=-1, keepdims=True)) + lam_init)
    @pl.when(pl.program_id(2) == 0)
    def _extend_values():
        vext_scr[:, :LANES] = v_ref[0, 0]
        vext_scr[:, LANES:] = jnp.ones((s, LANES), BF16)

    q = q_ref[0, 0]
    first = lax.broadcasted_iota(I32, (1, LANES), 1) < HEAD_DIM
    zero = jnp.zeros_like(q)
    qs = (jnp.where(first, q, zero), jnp.where(first, zero, q))
    nt = (((1,), (1,)), ((), ()))
    nchunk = s // tk
    mx = [jnp.full((tq, LANES), NEG_INF, F32), jnp.full((tq, LANES), NEG_INF, F32)]
    for c in range(nchunk):
        kc = k_ref[0, 0, c * tk:(c + 1) * tk, :]
        for u in range(2):
            sc = lax.dot_general(qs[u], kc, nt, preferred_element_type=F32)
            s_scr[u, :, c * tk:(c + 1) * tk] = sc
            for jj in range(tk // LANES):
                mx[u] = jnp.maximum(mx[u], sc[:, jj * LANES:(jj + 1) * LANES])
    mx = [jnp.max(m, axis=-1, keepdims=True) for m in mx]
    acc = [jnp.zeros((tq, 2 * LANES), F32), jnp.zeros((tq, 2 * LANES), F32)]
    for c in range(nchunk):
        vc = vext_scr[c * tk:(c + 1) * tk, :]
        for u in range(2):
            p = jnp.exp2(s_scr[u, :, c * tk:(c + 1) * tk] - mx[u]).astype(BF16)
            acc[u] = acc[u] + jnp.dot(p, vc, preferred_element_type=F32)
    c1 = 1.0 / acc[0][:, LANES:]
    c2 = lam / acc[1][:, LANES:]
    o = acc[0][:, :LANES] * c1 - acc[1][:, :LANES] * c2
    o_ref[0, 0] = ((_rms(o) * gsub_ref[...]) * (1.0 - lam_init)).astype(BF16)


def _attention(q, k, v, lamv, g_sub, lam_init, tq, tk):
    b, nh, s, _ = q.shape
    return pl.pallas_call(
        functools.partial(_attn_kernel, lam_init=lam_init, s=s, tq=tq, tk=tk),
        grid=(b, nh, s // tq),
        in_specs=[pl.BlockSpec((4, HEAD_DIM), lambda i, h, j: (0, 0)),
                  pl.BlockSpec((1, 1, tq, LANES), lambda i, h, j: (i, h, j, 0)),
                  pl.BlockSpec((1, 1, s, LANES), lambda i, h, j: (i, h, 0, 0)),
                  pl.BlockSpec((1, 1, s, LANES), lambda i, h, j: (i, h, 0, 0)),
                  pl.BlockSpec((1, LANES), lambda i, h, j: (0, 0))],
        out_specs=pl.BlockSpec((1, 1, tq, LANES), lambda i, h, j: (i, h, j, 0)),
        out_shape=jax.ShapeDtypeStruct((b, nh, s, LANES), BF16),
        scratch_shapes=[pltpu.VMEM((2, tq, s), F32), pltpu.VMEM((s, 2 * LANES), BF16)],
        compiler_params=_cparams(("parallel", "parallel", "arbitrary")),
        name="attn",
    )(lamv, q, k, v, g_sub)


def _merge_kernel(x_ref, mixed_ref, att_ref, sc1_ref, sh1_ref, gt1_ref, sc2_ref, sh2_ref,
                  g1_ref, g2_ref, wpool_ref, watt_ref, wgate_ref, bgate_ref, wout_ref,
                  x1_ref, h2_ref, *, d):
    x = x_ref[0]
    h = ((_rms(x) * g1_ref[...]) * (1.0 + sc1_ref[0]) + sh1_ref[0]).astype(BF16)
    gates = jax.nn.sigmoid(jnp.dot(h, wgate_ref[...], preferred_element_type=F32) + bgate_ref[...])
    y_pool = jnp.dot(mixed_ref[0], wpool_ref[...], preferred_element_type=F32)
    att = jnp.concatenate([att_ref[0, hd] for hd in range(d // LANES)], axis=1)
    y_att = jnp.dot(att, watt_ref[...], preferred_element_type=F32)
    merged = gates[:, :d] * y_pool + gates[:, d:] * y_att
    x1 = x + gt1_ref[0] * jnp.dot(merged.astype(BF16), wout_ref[...], preferred_element_type=F32)
    x1_ref[0] = x1
    h2 = (_rms(x1) * g2_ref[...]) * (1.0 + sc2_ref[0]) + sh2_ref[0]
    h2_ref[0] = h2.astype(BF16)


def _merge(x, mixed, att, mods, g1, g2, wpool, watt, wgate, bgate, wout, tm):
    b, s, d = x.shape
    pool_dim = mixed.shape[-1]
    row = lambda i, j: (i, j, 0)
    per_b = lambda i, j: (i, 0, 0)
    const = lambda i, j: (0, 0)
    sc1, sh1, gt1, sc2, sh2 = mods
    return pl.pallas_call(
        functools.partial(_merge_kernel, d=d),
        grid=(b, s // tm),
        in_specs=[pl.BlockSpec((1, tm, d), row),
                  pl.BlockSpec((1, tm, pool_dim), row),
                  pl.BlockSpec((1, d // LANES, tm, LANES), lambda i, j: (i, 0, j, 0))]
                 + [pl.BlockSpec((1, 1, d), per_b)] * 5
                 + [pl.BlockSpec((1, d), const), pl.BlockSpec((1, d), const),
                    pl.BlockSpec((pool_dim, d), const), pl.BlockSpec((d, d), const),
                    pl.BlockSpec((d, 2 * d), const), pl.BlockSpec((1, 2 * d), const),
                    pl.BlockSpec((d, d), const)],
        out_specs=[pl.BlockSpec((1, tm, d), row), pl.BlockSpec((1, tm, d), row)],
        out_shape=[jax.ShapeDtypeStruct((b, s, d), F32), jax.ShapeDtypeStruct((b, s, d), BF16)],
        compiler_params=_cparams(("parallel", "parallel")),
        name="merge",
    )(x, mixed, att, sc1, sh1, gt1, sc2, sh2, g1, g2, wpool, watt, wgate, bgate, wout)


ROUTE_TOKENS = 8 * LANES
LAY_PITCH = LANES + 8


def _sort16_network():
    n, pairs, p = TOPK, [], 1
    while p < n:
        k = p
        while k >= 1:
            for j in range(k % p, n - k, 2 * k):
                for i in range(min(k, n - j - k)):
                    if (i + j) // (2 * p) == (i + j + k) // (2 * p):
                        pairs.append((i + j, i + j + k))
            k //= 2
        p *= 2
    return tuple(pairs)


SORT16 = _sort16_network()


def _swap_mask(a, b):
    (av, ai), (bv, bi) = a, b
    return (bv > av) | ((bv == av) & (bi < ai))


def _order(a, b):
    sw = _swap_mask(a, b)
    first = (jnp.maximum(a[0], b[0]), jnp.where(sw, b[1], a[1]))
    second = (jnp.minimum(a[0], b[0]), jnp.where(sw, a[1], b[1]))
    return first, second


def _first_of(a, b):
    return jnp.maximum(a[0], b[0]), jnp.where(_swap_mask(a, b), b[1], a[1])


def _sort16(items):
    items = list(items)
    for i, j in SORT16:
        items[i], items[j] = _order(items[i], items[j])
    return items


def _merge_top16(a, b):
    c = [_first_of(a[i], b[TOPK - 1 - i]) for i in range(TOPK)]
    stride = TOPK // 2
    while stride:
        for i in range(TOPK):
            if not i & stride:
                c[i], c[i + stride] = _order(c[i], c[i + stride])
        stride //= 2
    return c


def _route_kernel(h2_ref, wqt_ref, keys_ref, e1_ref, e2_ref, gate_ref,
                  qt_scr, lay_scr, tv_scr, ti_scr):
    nblk = ROUTE_TOKENS // LANES
    qt = lax.dot_general(wqt_ref[...], h2_ref[...], (((1,), (1,)), ((), ())),
                         preferred_element_type=F32)
    qt_scr[...] = qt.astype(BF16)

    def set_body(h, carry):
        for part in range(2):
            si = 2 * h + part
            r0 = pl.multiple_of(si * HEAD_DIM, HEAD_DIM)
            st = jnp.dot(keys_ref[si], qt_scr[pl.ds(r0, HEAD_DIM), :],
                         preferred_element_type=F32)
            for c in range(nblk):
                row = (part * nblk + c) * LAY_PITCH
                lay_scr[row:row + LANES, :] = st[:, c * LANES:(c + 1) * LANES]
        lists = []
        for g in range(LANES // TOPK):
            items = []
            for k in range(g * TOPK, (g + 1) * TOPK):
                v = jnp.stack([lay_scr[pl.ds(part * nblk * LAY_PITCH + k, nblk, stride=LAY_PITCH), :]
                               for part in range(2)])
                items.append((v, jnp.full((2, nblk, LANES), k, I32)))
            lists.append(_sort16(items))
        while len(lists) > 1:
            lists = [_merge_top16(lists[2 * i], lists[2 * i + 1]) for i in range(len(lists) // 2)]
        for a, (v, idx) in enumerate(lists[0]):
            for part in range(2):
                tv_scr[(2 * h + part) * TOPK + a] = v[part]
                ti_scr[(2 * h + part) * TOPK + a] = idx[part]
        return carry

    lax.fori_loop(0, N_HEADS, set_body, 0)

    def head_body(h, carry):
        b1 = 2 * h * TOPK
        b2 = b1 + TOPK
        cv, ce = [], []
        for a, nb in CAND_ROWS:
            va = tv_scr[b1 + a]
            ea = ti_scr[b1 + a] * LANES
            for b in range(nb):
                cv.append(va + tv_scr[b2 + b])
                ce.append(ea + ti_scr[b2 + b])
        best_s, best_e = [], []
        for _ in range(TOPK):
            nodes = list(zip(cv, ce))
            while len(nodes) > 1:
                nxt = []
                for i in range(0, len(nodes) - 1, 2):
                    (lv, le), (rv, re) = nodes[i], nodes[i + 1]
                    nxt.append((jnp.maximum(lv, rv), jnp.where(rv > lv, re, le)))
                if len(nodes) % 2:
                    nxt.append(nodes[-1])
                nodes = nxt
            m, e = nodes[0]
            best_s.append(m)
            best_e.append(e)
            cv = [jnp.where(c_e == e, NEG_INF, c_v) for c_v, c_e in zip(cv, ce)]
        ex = [jnp.exp(b - best_s[0]) for b in best_s]
        z = ex[0]
        for t in ex[1:]:
            z = z + t
        for k in range(TOPK):
            e1_ref[h * TOPK + k] = lax.shift_right_logical(best_e[k], 7)
            e2_ref[h * TOPK + k] = best_e[k] & (LANES - 1)
            gate_ref[h * TOPK + k] = ex[k] / z
        return carry

    lax.fori_loop(0, N_HEADS, head_body, 0)


def _route(h2_flat, wqt_bf, keys_bf):
    t, d = h2_flat.shape
    assert t % ROUTE_TOKENS == 0
    nslot = N_HEADS * TOPK
    nblk = ROUTE_TOKENS // LANES
    keys2 = keys_bf.reshape(2 * N_HEADS, LANES, HEAD_DIM)
    out = pl.BlockSpec((nslot, nblk, LANES), lambda i: (0, i, 0))
    outs = pl.pallas_call(
        _route_kernel,
        grid=(t // ROUTE_TOKENS,),
        in_specs=[pl.BlockSpec((ROUTE_TOKENS, d), lambda i: (i, 0)),
                  pl.BlockSpec(wqt_bf.shape, lambda i: (0, 0)),
                  pl.BlockSpec(keys2.shape, lambda i: (0, 0, 0))],
        out_specs=[out, out, out],
        out_shape=[jax.ShapeDtypeStruct((nslot, t // LANES, LANES), I32),
                   jax.ShapeDtypeStruct((nslot, t // LANES, LANES), I32),
                   jax.ShapeDtypeStruct((nslot, t // LANES, LANES), F32)],
        scratch_shapes=[pltpu.VMEM((wqt_bf.shape[0], ROUTE_TOKENS), BF16),
                        pltpu.VMEM((2 * nblk * LAY_PITCH, LANES), F32),
                        pltpu.VMEM((2 * N_HEADS * TOPK, nblk, LANES), F32),
                        pltpu.VMEM((2 * N_HEADS * TOPK, nblk, LANES), I32)],
        compiler_params=_cparams(("parallel",)),
        name="route",
    )(h2_flat, wqt_bf, keys2)
    e1, e2, gate = [o.reshape(nslot, t) for o in outs]
    return e1.T, e2.T, e2, gate.T


GATE_ROWS = LANES // 2
GATE_PITCH = GATE_ROWS + 4
SUBLANES = 8
BUILD_GROUP = 128
HI_MASK = -65536
PEER_CHUNK = 2048
PEER_TOKENS = 512


def _peer_kernel(h2_ref, e1_ref, e2_ref, e2t_ref, gate_ref, ut_ref, v_ref, x1_ref, gt2_ref, gf_ref,
                 y_ref, gmat_scr, w_scr, *, tp, ec):
    c = pl.program_id(1)
    nsub = ec // LANES

    @pl.when(c == 0)
    def _build_gate_matrices():
        y_ref[...] = jnp.zeros_like(y_ref)
        key_id = lax.broadcasted_iota(I32, (LANES, LANES), 0)

        lane_id = lax.broadcasted_iota(I32, (LANES, LANES), 1)

        def body(tb, carry):
            lane0 = pl.multiple_of(tb * BUILD_GROUP, BUILD_GROUP)
            cols = e2t_ref[:, pl.ds(lane0, BUILD_GROUP)]
            for sub in range(BUILD_GROUP // SUBLANES):
                base = pl.multiple_of(lane0 + sub * SUBLANES, SUBLANES)
                r1 = e1_ref[pl.ds(base, SUBLANES), :]
                r2 = e2_ref[pl.ds(base, SUBLANES), :]
                gg = gate_ref[pl.ds(base, SUBLANES), :]
                for i in range(SUBLANES):
                    t = sub * SUBLANES + i
                    a = jnp.where(key_id == r1[i:i + 1, :], gg[i:i + 1, :], 0.0).astype(BF16)
                    if i % 2:
                        b = jnp.where(key_id == r2[i:i + 1, :], 1.0, 0.0).astype(BF16)
                        gm = lax.dot_general(a, b, (((1,), (1,)), ((), ())),
                                             preferred_element_type=F32)
                    else:
                        bt = jnp.where(lane_id == cols[:, t:t + 1], 1.0, 0.0).astype(BF16)
                        gm = jnp.dot(a, bt, preferred_element_type=F32)
                    gmat_scr[pl.ds((base + i) * GATE_PITCH, GATE_ROWS), :] = pltpu.bitcast(
                        gm.astype(BF16), I32)
            return carry

        lax.fori_loop(0, tp // BUILD_GROUP, body, 0)

    h2 = h2_ref[...]
    for jb in range(ec // MXU_DEPTH):
        lo = jb * MXU_DEPTH
        s = jnp.dot(h2, ut_ref[0, :, lo:lo + MXU_DEPTH], preferred_element_type=F32)
        for u in range(MXU_DEPTH // LANES):
            k1 = c * nsub + jb * (MXU_DEPTH // LANES) + u
            word = gmat_scr[pl.ds(k1 // 2, tp, stride=GATE_PITCH), :]
            shift = (1 - k1 % 2) * 16
            gi = lax.bitcast_convert_type(lax.shift_left(word, shift) & HI_MASK, F32)
            su = s[:, u * LANES:(u + 1) * LANES]
            w_scr[:, lo + u * LANES:lo + (u + 1) * LANES] = (
                (0.5 * su) * (1.0 + lax.erf(su * INV_SQRT2)) * gi).astype(BF16)
    y_ref[...] += jnp.dot(w_scr[...], v_ref[...], preferred_element_type=F32)

    @pl.when(c == pl.num_programs(1) - 1)
    def _finish():
        xo = x1_ref[...] + gt2_ref[0] * y_ref[...]
        y_ref[...] = _rms(xo) * gf_ref[...]


def _peer(h2_flat, e1, e2, e2t, gate, ut_bf, v_bf, x1_flat, gt2, g_final, tokens_per_batch, tp):
    t, d = h2_flat.shape
    ec = ut_bf.shape[2]
    nc = v_bf.shape[0] // ec
    nslot = e1.shape[1]
    tiles_per_batch = tokens_per_batch // tp
    tok = lambda i, c: (i, 0)
    once = pl.Buffered(1)
    return pl.pallas_call(
        functools.partial(_peer_kernel, tp=tp, ec=ec),
        grid=(t // tp, nc),
        in_specs=[pl.BlockSpec((tp, d), tok),
                  pl.BlockSpec((tp, nslot), tok),
                  pl.BlockSpec((tp, nslot), tok),
                  pl.BlockSpec((nslot, tp), lambda i, c: (0, i)),
                  pl.BlockSpec((tp, nslot), tok),
                  pl.BlockSpec((1, d, ec), lambda i, c: (c, 0, 0)),
                  pl.BlockSpec((ec, d), lambda i, c: (c, 0)),
                  pl.BlockSpec((tp, d), tok, pipeline_mode=once),
                  pl.BlockSpec((1, 1, d), lambda i, c: (i // tiles_per_batch, 0, 0)),
                  pl.BlockSpec((1, d), lambda i, c: (0, 0))],
        out_specs=pl.BlockSpec((tp, d), tok),
        out_shape=jax.ShapeDtypeStruct((t, d), F32),
        scratch_shapes=[pltpu.VMEM((tp * GATE_PITCH, LANES), I32),
                        pltpu.VMEM((tp, ec), BF16)],
        compiler_params=_cparams(("parallel", "arbitrary")),
        name="peer",
    )(h2_flat, e1, e2, e2t, gate, ut_bf, v_bf, x1_flat, gt2, g_final)


def _rope_tables(s):
    inv_freq = 1.0 / (ROPE_THETA ** (jnp.arange(0, HEAD_DIM, 2, dtype=F32) / HEAD_DIM))
    ang = jnp.arange(s, dtype=F32)[:, None] * inv_freq[None, :]
    ang = jnp.concatenate([ang, ang, ang, ang], axis=-1)
    cos, sin = jnp.cos(ang), jnp.sin(ang)
    low = (jnp.arange(LANES) % HEAD_DIM) < (HEAD_DIM // 2)
    return cos, jnp.where(low, -sin, 0.0), jnp.where(low, 0.0, sin)


def _pick_tile(n, want):
    t = min(n, want)
    while n % t:
        t //= 2
    return t


def _trunk(x, mod, wts, lam_init):
    b, s, d = x.shape
    sh1, sc1, gt1, sh2, sc2, gt2 = [m.reshape(b, 1, d) for m in jnp.split(mod, 6, axis=-1)]
    cos, sa, sb = _rope_tables(s)
    p, q, k, v = _front(x, sc1, sh1, wts["g_norm1"], wts["w_in"], cos, sa, sb, _pick_tile(s, 512))
    mixed = _pool(p, wts["pool_w"], wts["pool_scale"])
    att = _attention(q, k, v, wts["lamv"], wts["g_sub"], lam_init,
                     _pick_tile(s, ATTN_SCORE_BYTES // (2 * 4 * s)),
                     _pick_tile(s, 512))
    x1, h2 = _merge(x, mixed, att, (sc1, sh1, gt1, sc2, sh2), wts["g_norm1"], wts["g_norm2"],
                    wts["w_pool_out"], wts["w_att_out"], wts["w_gate"], wts["b_gate"],
                    wts["w_out"], _pick_tile(s, 512))
    t = b * s
    h2f = h2.reshape(t, d)
    e1, e2, e2t, gate = _route(h2f, wts["w_peer_qt"], wts["peer_keys"])
    y = _peer(h2f, e1, e2, e2t, gate, wts["peer_ut"], wts["peer_v"], x1.reshape(t, d),
              gt2, wts["g_final"], s, _pick_tile(s, PEER_TOKENS))
    return y.reshape(b, s, d)


def kernel(x_prompt, x_sample, c_prompt, c_sample, w_ada, b_ada, g_norm1, w_in, pool_w, pool_scale,
           w_pool_out, lam_q1, lam_k1, lam_q2, lam_k2, g_sub, w_att_out, w_gate, b_gate, w_out,
           g_norm2, w_peer_q, peer_keys, peer_u, peer_v, g_final):
    depth = w_ada.shape[0]
    assert depth == 1, "single-layer trunk"
    l = 0
    lam_init = 0.8 - 0.6 * math.exp(-0.3 * l)
    d = x_prompt.shape[-1]
    wts = {
        "g_norm1": g_norm1[l][None, :], "g_norm2": g_norm2[l][None, :], "g_final": g_final[None, :],
        "w_in": w_in[l].astype(BF16),
        "pool_w": pool_w[l].astype(BF16), "pool_scale": pool_scale[l][None, :],
        "w_pool_out": w_pool_out[l].astype(BF16),
        "lamv": jnp.stack([lam_q1[l], lam_k1[l], lam_q2[l], lam_k2[l]]).astype(F32),
        "g_sub": g_sub[l][None, :],
        "w_att_out": w_att_out[l].astype(BF16),
        "w_gate": w_gate[l].astype(BF16), "b_gate": b_gate[l][None, :],
        "w_out": w_out[l].astype(BF16),
        "w_peer_qt": w_peer_q[l].T.astype(BF16),
        "peer_keys": peer_keys[l].astype(BF16),
        "peer_ut": peer_u[l].astype(BF16).reshape(-1, PEER_CHUNK, d).transpose(0, 2, 1),
        "peer_v": peer_v[l].astype(BF16),
    }
    nb = x_prompt.shape[0]
    mod = _ada(jnp.concatenate([c_prompt, c_sample], axis=0), w_ada[l].astype(BF16), b_ada[l][None, :])
    y_prompt = _trunk(x_prompt, mod[:nb], wts, lam_init)
    y_sample = _trunk(x_sample, mod[nb:], wts, lam_init)
    return (y_prompt, y_sample)
```

```python
import functools
import math

import jax
import jax.numpy as jnp
from jax import lax
from jax.experimental import pallas as pl
from jax.experimental.pallas import tpu as pltpu

F32 = jnp.float32
BF16 = jnp.bfloat16
I32 = jnp.int32

NORM_EPS = 1e-6
ROPE_THETA = 10000.0
POOL_WINDOWS = (2, 4, 8, 16)
POOL_PAD = 16
N_HEADS = 8
HEAD_DIM = 64
LANES = 128
MXU_DEPTH = 256
TOPK = 16
CAND_ROWS = tuple((a, TOPK // (a + 1)) for a in range(TOPK))
VMEM_LIMIT = 56 * 1024 * 1024
ATTN_SCORE_BYTES = 16 * 1024 * 1024
INV_SQRT2 = 1.0 / math.sqrt(2.0)
LOG2E = 1.0 / math.log(2.0)
NEG_INF = float("-inf")


def _cparams(sem):
    return pltpu.CompilerParams(dimension_semantics=sem, vmem_limit_bytes=VMEM_LIMIT)


def _rms(x):
    return x * lax.rsqrt(jnp.mean(x * x, axis=-1, keepdims=True) + NORM_EPS)


def _ada_kernel(c_ref, w_ref, b_ref, o_ref):
    c = c_ref[...]
    a = (c * jax.nn.sigmoid(c)).astype(BF16)
    o_ref[...] = jnp.dot(a, w_ref[...], preferred_element_type=F32) + b_ref[...]


def _ada(c, w_bf, b):
    n, d = c.shape
    nout = w_bf.shape[1]
    return pl.pallas_call(
        _ada_kernel,
        grid=(nout // d,),
        in_specs=[pl.BlockSpec((n, d), lambda j: (0, 0)),
                  pl.BlockSpec((d, d), lambda j: (0, j)),
                  pl.BlockSpec((1, d), lambda j: (0, j))],
        out_specs=pl.BlockSpec((n, d), lambda j: (0, j)),
        out_shape=jax.ShapeDtypeStruct((n, nout), F32),
        compiler_params=_cparams(("parallel",)),
        name="ada",
    )(c, w_bf, b)


def _front_kernel(x_ref, sc_ref, sh_ref, g_ref, w_ref, cos_ref, sa_ref, sb_ref,
                  p_ref, q_ref, k_ref, v_ref, *, d, pool_dim):
    x = x_ref[0]
    h = (_rms(x) * g_ref[...]) * (1.0 + sc_ref[0]) + sh_ref[0]
    proj = jnp.dot(h.astype(BF16), w_ref[...], preferred_element_type=F32)
    p_ref[0] = proj[:, :pool_dim]
    cos, sa, sb = cos_ref[...], sa_ref[...], sb_ref[...]
    half = HEAD_DIM // 2

    def rope(t):
        return (t * cos + pltpu.roll(t, LANES - half, axis=1) * sa
                + pltpu.roll(t, half, axis=1) * sb)

    qscale = HEAD_DIM ** -0.5 * LOG2E
    for j in range(d // LANES):
        o = pool_dim + j * LANES
        q_ref[0, j] = (rope(proj[:, o:o + LANES]) * qscale).astype(BF16)
        o = pool_dim + d + j * LANES
        k_ref[0, j] = rope(proj[:, o:o + LANES]).astype(BF16)
        o = pool_dim + 2 * d + j * LANES
        v_ref[0, j] = proj[:, o:o + LANES].astype(BF16)


def _front(x, sc, sh, g, w_in_bf, cos, sa, sb, tm):
    b, s, d = x.shape
    in_dim = w_in_bf.shape[1]
    pool_dim = in_dim - 3 * d
    nh = d // LANES
    row = lambda i, j: (i, j, 0)
    per_b = lambda i, j: (i, 0, 0)
    const = lambda i, j: (0, 0)
    return pl.pallas_call(
        functools.partial(_front_kernel, d=d, pool_dim=pool_dim),
        grid=(b, s // tm),
        in_specs=[pl.BlockSpec((1, tm, d), row),
                  pl.BlockSpec((1, 1, d), per_b),
                  pl.BlockSpec((1, 1, d), per_b),
                  pl.BlockSpec((1, d), const),
                  pl.BlockSpec((d, in_dim), const),
                  pl.BlockSpec((tm, LANES), lambda i, j: (j, 0)),
                  pl.BlockSpec((tm, LANES), lambda i, j: (j, 0)),
                  pl.BlockSpec((tm, LANES), lambda i, j: (j, 0))],
        out_specs=[pl.BlockSpec((1, tm, pool_dim), row)]
                  + [pl.BlockSpec((1, nh, tm, LANES), lambda i, j: (i, 0, j, 0))] * 3,
        out_shape=[jax.ShapeDtypeStruct((b, s, pool_dim), F32)]
                  + [jax.ShapeDtypeStruct((b, nh, s, LANES), BF16)] * 3,
        compiler_params=_cparams(("parallel", "parallel")),
        name="front",
    )(x, sc, sh, g, w_in_bf, cos, sa, sb)


def _pool_kernel(p_ref, w_ref, scale_ref, o_ref, pad_ref, *, s):
    pos = lax.broadcasted_iota(I32, (s, LANES), 0)
    zeros = jnp.zeros((POOL_PAD, LANES), F32)
    for g, win in enumerate(POOL_WINDOWS):
        half = win // 2
        xg = p_ref[0, :, g * LANES:(g + 1) * LANES]
        pad_ref[0:POOL_PAD, :] = zeros
        pad_ref[POOL_PAD:POOL_PAD + s, :] = xg
        pad_ref[POOL_PAD + s:2 * POOL_PAD + s, :] = zeros
        wsum = pad_ref[POOL_PAD - half:POOL_PAD - half + s, :]
        for dlt in range(-half + 1, half):
            wsum = wsum + pad_ref[POOL_PAD + dlt:POOL_PAD + dlt + s, :]
        cnt = (jnp.minimum(pos + half, s) - jnp.maximum(pos - half, 0)).astype(F32)
        diff = wsum / cnt - xg
        mixed = jnp.dot(diff.astype(BF16), w_ref[g], preferred_element_type=F32)
        o_ref[0, :, g * LANES:(g + 1) * LANES] = (
            mixed * scale_ref[:, g * LANES:(g + 1) * LANES]).astype(BF16)


def _pool(p, pool_w_bf, pool_scale):
    b, s, pool_dim = p.shape
    ng = pool_w_bf.shape[0]
    return pl.pallas_call(
        functools.partial(_pool_kernel, s=s),
        grid=(b,),
        in_specs=[pl.BlockSpec((1, s, pool_dim), lambda i: (i, 0, 0)),
                  pl.BlockSpec((ng, LANES, LANES), lambda i: (0, 0, 0)),
                  pl.BlockSpec((1, pool_dim), lambda i: (0, 0))],
        out_specs=pl.BlockSpec((1, s, pool_dim), lambda i: (i, 0, 0)),
        out_shape=jax.ShapeDtypeStruct((b, s, pool_dim), BF16),
        scratch_shapes=[pltpu.VMEM((s + 2 * POOL_PAD, LANES), F32)],
        compiler_params=_cparams(("parallel",)),
        name="pool",
    )(p, pool_w_bf, pool_scale)


def _attn_kernel(lamv_ref, q_ref, k_ref, v_ref, gsub_ref, o_ref, s_scr, vext_scr,
                 *, lam_init, s, tq, tk):
    lv = lamv_ref[...]
    lam = (jnp.exp(jnp.sum(lv[0:1] * lv[1:2], axis=-1, keepdims=True))
           - jnp.exp(jnp.sum(lv[2:3] * lv[3:4], axis=-1, keepdims=True)) + lam_init)
    @pl.when(pl.program_id(2) == 0)
    def _extend_values():
        vext_scr[:, :LANES] = v_ref[0, 0]
        vext_scr[:, LANES:] = jnp.ones((s, LANES), BF16)

    q = q_ref[0, 0]
    first = lax.broadcasted_iota(I32, (1, LANES), 1) < HEAD_DIM
    zero = jnp.zeros_like(q)
    qs = (jnp.where(first, q, zero), jnp.where(first, zero, q))
    nt = (((1,), (1,)), ((), ()))
    nchunk = s // tk
    mx = [jnp.full((tq, LANES), NEG_INF, F32), jnp.full((tq, LANES), NEG_INF, F32)]
    for c in range(nchunk):
        kc = k_ref[0, 0, c * tk:(c + 1) * tk, :]
        for u in range(2):
            sc = lax.dot_general(qs[u], kc, nt, preferred_element_type=F32)
            s_scr[u, :, c * tk:(c + 1) * tk] = sc
            for jj in range(tk // LANES):
                mx[u] = jnp.maximum(mx[u], sc[:, jj * LANES:(jj + 1) * LANES])
    mx = [jnp.max(m, axis=-1, keepdims=True) for m in mx]
    acc = [jnp.zeros((tq, 2 * LANES), F32), jnp.zeros((tq, 2 * LANES), F32)]
    for c in range(nchunk):
        vc = vext_scr[c * tk:(c + 1) * tk, :]
        for u in range(2):
            p = jnp.exp2(s_scr[u, :, c * tk:(c + 1) * tk] - mx[u]).astype(BF16)
            acc[u] = acc[u] + jnp.dot(p, vc, preferred_element_type=F32)
    c1 = 1.0 / acc[0][:, LANES:]
    c2 = lam / acc[1][:, LANES:]
    o = acc[0][:, :LANES] * c1 - acc[1][:, :LANES] * c2
    o_ref[0, 0] = ((_rms(o) * gsub_ref[...]) * (1.0 - lam_init)).astype(BF16)


def _attention(q, k, v, lamv, g_sub, lam_init, tq, tk):
    b, nh, s, _ = q.shape
    return pl.pallas_call(
        functools.partial(_attn_kernel, lam_init=lam_init, s=s, tq=tq, tk=tk),
        grid=(b, nh, s // tq),
        in_specs=[pl.BlockSpec((4, HEAD_DIM), lambda i, h, j: (0, 0)),
                  pl.BlockSpec((1, 1, tq, LANES), lambda i, h, j: (i, h, j, 0)),
                  pl.BlockSpec((1, 1, s, LANES), lambda i, h, j: (i, h, 0, 0)),
                  pl.BlockSpec((1, 1, s, LANES), lambda i, h, j: (i, h, 0, 0)),
                  pl.BlockSpec((1, LANES), lambda i, h, j: (0, 0))],
        out_specs=pl.BlockSpec((1, 1, tq, LANES), lambda i, h, j: (i, h, j, 0)),
        out_shape=jax.ShapeDtypeStruct((b, nh, s, LANES), BF16),
        scratch_shapes=[pltpu.VMEM((2, tq, s), F32), pltpu.VMEM((s, 2 * LANES), BF16)],
        compiler_params=_cparams(("parallel", "parallel", "arbitrary")),
        name="attn",
    )(lamv, q, k, v, g_sub)


def _merge_kernel(x_ref, mixed_ref, att_ref, sc1_ref, sh1_ref, gt1_ref, sc2_ref, sh2_ref,
                  g1_ref, g2_ref, wpool_ref, watt_ref, wgate_ref, bgate_ref, wout_ref,
                  x1_ref, h2_ref, *, d):
    x = x_ref[0]
    h = ((_rms(x) * g1_ref[...]) * (1.0 + sc1_ref[0]) + sh1_ref[0]).astype(BF16)
    gates = jax.nn.sigmoid(jnp.dot(h, wgate_ref[...], preferred_element_type=F32) + bgate_ref[...])
    y_pool = jnp.dot(mixed_ref[0], wpool_ref[...], preferred_element_type=F32)
    att = jnp.concatenate([att_ref[0, hd] for hd in range(d // LANES)], axis=1)
    y_att = jnp.dot(att, watt_ref[...], preferred_element_type=F32)
    merged = gates[:, :d] * y_pool + gates[:, d:] * y_att
    x1 = x + gt1_ref[0] * jnp.dot(merged.astype(BF16), wout_ref[...], preferred_element_type=F32)
    x1_ref[0] = x1
    h2 = (_rms(x1) * g2_ref[...]) * (1.0 + sc2_ref[0]) + sh2_ref[0]
    h2_ref[0] = h2.astype(BF16)


def _merge(x, mixed, att, mods, g1, g2, wpool, watt, wgate, bgate, wout, tm):
    b, s, d = x.shape
    pool_dim = mixed.shape[-1]
    row = lambda i, j: (i, j, 0)
    per_b = lambda i, j: (i, 0, 0)
    const = lambda i, j: (0, 0)
    sc1, sh1, gt1, sc2, sh2 = mods
    return pl.pallas_call(
        functools.partial(_merge_kernel, d=d),
        grid=(b, s // tm),
        in_specs=[pl.BlockSpec((1, tm, d), row),
                  pl.BlockSpec((1, tm, pool_dim), row),
                  pl.BlockSpec((1, d // LANES, tm, LANES), lambda i, j: (i, 0, j, 0))]
                 + [pl.BlockSpec((1, 1, d), per_b)] * 5
                 + [pl.BlockSpec((1, d), const), pl.BlockSpec((1, d), const),
                    pl.BlockSpec((pool_dim, d), const), pl.BlockSpec((d, d), const),
                    pl.BlockSpec((d, 2 * d), const), pl.BlockSpec((1, 2 * d), const),
                    pl.BlockSpec((d, d), const)],
        out_specs=[pl.BlockSpec((1, tm, d), row), pl.BlockSpec((1, tm, d), row)],
        out_shape=[jax.ShapeDtypeStruct((b, s, d), F32), jax.ShapeDtypeStruct((b, s, d), BF16)],
        compiler_params=_cparams(("parallel", "parallel")),
        name="merge",
    )(x, mixed, att, sc1, sh1, gt1, sc2, sh2, g1, g2, wpool, watt, wgate, bgate, wout)


ROUTE_TOKENS = 8 * LANES
LAY_PITCH = LANES + 8


def _sort16_network():
    n, pairs, p = TOPK, [], 1
    while p < n:
        k = p
        while k >= 1:
            for j in range(k % p, n - k, 2 * k):
                for i in range(min(k, n - j - k)):
                    if (i + j) // (2 * p) == (i + j + k) // (2 * p):
                        pairs.append((i + j, i + j + k))
            k //= 2
        p *= 2
    return tuple(pairs)


SORT16 = _sort16_network()


def _swap_mask(a, b):
    (av, ai), (bv, bi) = a, b
    return (bv > av) | ((bv == av) & (bi < ai))


def _order(a, b):
    sw = _swap_mask(a, b)
    first = (jnp.maximum(a[0], b[0]), jnp.where(sw, b[1], a[1]))
    second = (jnp.minimum(a[0], b[0]), jnp.where(sw, a[1], b[1]))
    return first, second


def _first_of(a, b):
    return jnp.maximum(a[0], b[0]), jnp.where(_swap_mask(a, b), b[1], a[1])


def _sort16(items):
    items = list(items)
    for i, j in SORT16:
        items[i], items[j] = _order(items[i], items[j])
    return items


def _merge_top16(a, b):
    c = [_first_of(a[i], b[TOPK - 1 - i]) for i in range(TOPK)]
    stride = TOPK // 2
    while stride:
        for i in range(TOPK):
            if not i & stride:
                c[i], c[i + stride] = _order(c[i], c[i + stride])
        stride //= 2
    return c


def _route_kernel(h2_ref, wqt_ref, keys_ref, e1_ref, e2_ref, gate_ref,
                  qt_scr, lay_scr, tv_scr, ti_scr):
    nblk = ROUTE_TOKENS // LANES
    qt = lax.dot_general(wqt_ref[...], h2_ref[...], (((1,), (1,)), ((), ())),
                         preferred_element_type=F32)
    qt_scr[...] = qt.astype(BF16)

    def set_body(h, carry):
        for part in range(2):
            si = 2 * h + part
            r0 = pl.multiple_of(si * HEAD_DIM, HEAD_DIM)
            st = jnp.dot(keys_ref[si], qt_scr[pl.ds(r0, HEAD_DIM), :],
                         preferred_element_type=F32)
            for c in range(nblk):
                row = (part * nblk + c) * LAY_PITCH
                lay_scr[row:row + LANES, :] = st[:, c * LANES:(c + 1) * LANES]
        lists = []
        for g in range(LANES // TOPK):
            items = []
            for k in range(g * TOPK, (g + 1) * TOPK):
                v = jnp.stack([lay_scr[pl.ds(part * nblk * LAY_PITCH + k, nblk, stride=LAY_PITCH), :]
                               for part in range(2)])
                items.append((v, jnp.full((2, nblk, LANES), k, I32)))
            lists.append(_sort16(items))
        while len(lists) > 1:
            lists = [_merge_top16(lists[2 * i], lists[2 * i + 1]) for i in range(len(lists) // 2)]
        for a, (v, idx) in enumerate(lists[0]):
            for part in range(2):
                tv_scr[(2 * h + part) * TOPK + a] = v[part]
                ti_scr[(2 * h + part) * TOPK + a] = idx[part]
        return carry

    lax.fori_loop(0, N_HEADS, set_body, 0)

    def head_body(h, carry):
        b1 = 2 * h * TOPK
        b2 = b1 + TOPK
        cv, ce = [], []
        for a, nb in CAND_ROWS:
            va = tv_scr[b1 + a]
            ea = ti_scr[b1 + a] * LANES
            for b in range(nb):
                cv.append(va + tv_scr[b2 + b])
                ce.append(ea + ti_scr[b2 + b])
        best_s, best_e = [], []
        for _ in range(TOPK):
            nodes = list(zip(cv, ce))
            while len(nodes) > 1:
                nxt = []
                for i in range(0, len(nodes) - 1, 2):
                    (lv, le), (rv, re) = nodes[i], nodes[i + 1]
                    nxt.append((jnp.maximum(lv, rv), jnp.where(rv > lv, re, le)))
                if len(nodes) % 2:
                    nxt.append(nodes[-1])
                nodes = nxt
            m, e = nodes[0]
            best_s.append(m)
            best_e.append(e)
            cv = [jnp.where(c_e == e, NEG_INF, c_v) for c_v, c_e in zip(cv, ce)]
        ex = [jnp.exp(b - best_s[0]) for b in best_s]
        z = ex[0]
        for t in ex[1:]:
            z = z + t
        for k in range(TOPK):
            e1_ref[h * TOPK + k] = lax.shift_right_logical(best_e[k], 7)
            e2_ref[h * TOPK + k] = best_e[k] & (LANES - 1)
            gate_ref[h * TOPK + k] = ex[k] / z
        return carry

    lax.fori_loop(0, N_HEADS, head_body, 0)


def _route(h2_flat, wqt_bf, keys_bf):
    t, d = h2_flat.shape
    assert t % ROUTE_TOKENS == 0
    nslot = N_HEADS * TOPK
    nblk = ROUTE_TOKENS // LANES
    keys2 = keys_bf.reshape(2 * N_HEADS, LANES, HEAD_DIM)
    out = pl.BlockSpec((nslot, nblk, LANES), lambda i: (0, i, 0))
    outs = pl.pallas_call(
        _route_kernel,
        grid=(t // ROUTE_TOKENS,),
        in_specs=[pl.BlockSpec((ROUTE_TOKENS, d), lambda i: (i, 0)),
                  pl.BlockSpec(wqt_bf.shape, lambda i: (0, 0)),
                  pl.BlockSpec(keys2.shape, lambda i: (0, 0, 0))],
        out_specs=[out, out, out],
        out_shape=[jax.ShapeDtypeStruct((nslot, t // LANES, LANES), I32),
                   jax.ShapeDtypeStruct((nslot, t // LANES, LANES), I32),
                   jax.ShapeDtypeStruct((nslot, t // LANES, LANES), F32)],
        scratch_shapes=[pltpu.VMEM((wqt_bf.shape[0], ROUTE_TOKENS), BF16),
                        pltpu.VMEM((2 * nblk * LAY_PITCH, LANES), F32),
                        pltpu.VMEM((2 * N_HEADS * TOPK, nblk, LANES), F32),
                        pltpu.VMEM((2 * N_HEADS * TOPK, nblk, LANES), I32)],
        compiler_params=_cparams(("parallel",)),
        name="route",
    )(h2_flat, wqt_bf, keys2)
    e1, e2, gate = [o.reshape(nslot, t) for o in outs]
    return e1.T, e2.T, e2, gate.T


GATE_ROWS = LANES // 2
GATE_PITCH = GATE_ROWS + 4
SUBLANES = 8
BUILD_GROUP = 128
HI_MASK = -65536
PEER_CHUNK = 2048
PEER_TOKENS = 512


def _peer_kernel(h2_ref, e1_ref, e2_ref, e2t_ref, gate_ref, ut_ref, v_ref, x1_ref, gt2_ref, gf_ref,
                 y_ref, gmat_scr, w_scr, *, tp, ec):
    c = pl.program_id(1)
    nsub = ec // LANES

    @pl.when(c == 0)
    def _build_gate_matrices():
        y_ref[...] = jnp.zeros_like(y_ref)
        key_id = lax.broadcasted_iota(I32, (LANES, LANES), 0)

        lane_id = lax.broadcasted_iota(I32, (LANES, LANES), 1)

        def body(tb, carry):
            lane0 = pl.multiple_of(tb * BUILD_GROUP, BUILD_GROUP)
            cols = e2t_ref[:, pl.ds(lane0, BUILD_GROUP)]
            for sub in range(BUILD_GROUP // SUBLANES):
                base = pl.multiple_of(lane0 + sub * SUBLANES, SUBLANES)
                r1 = e1_ref[pl.ds(base, SUBLANES), :]
                r2 = e2_ref[pl.ds(base, SUBLANES), :]
                gg = 0.5 * gate_ref[pl.ds(base, SUBLANES), :]
                for i in range(SUBLANES):
                    t = sub * SUBLANES + i
                    a = jnp.where(key_id == r1[i:i + 1, :], gg[i:i + 1, :], 0.0).astype(BF16)
                    if i % 3 == 1:
                        b = jnp.where(key_id == r2[i:i + 1, :], 1.0, 0.0).astype(BF16)
                        gm = lax.dot_general(a, b, (((1,), (1,)), ((), ())),
                                             preferred_element_type=F32)
                    else:
                        bt = jnp.where(lane_id == cols[:, t:t + 1], 1.0, 0.0).astype(BF16)
                        gm = jnp.dot(a, bt, preferred_element_type=F32)
                    gmat_scr[pl.ds((base + i) * GATE_PITCH, GATE_ROWS), :] = pltpu.bitcast(
                        gm.astype(BF16), I32)
            return carry

        lax.fori_loop(0, tp // BUILD_GROUP, body, 0)

    h2 = h2_ref[...]
    for jb in range(ec // MXU_DEPTH):
        lo = jb * MXU_DEPTH
        s = jnp.dot(h2, ut_ref[0, :, lo:lo + MXU_DEPTH], preferred_element_type=F32)
        for u in range(MXU_DEPTH // LANES):
            k1 = c * nsub + jb * (MXU_DEPTH // LANES) + u
            word = gmat_scr[pl.ds(k1 // 2, tp, stride=GATE_PITCH), :]
            half_gate = lax.bitcast_convert_type(
                word & HI_MASK if u % 2 else lax.shift_left(word, 16), F32)
            su = s[:, u * LANES:(u + 1) * LANES]
            w_scr[:, lo + u * LANES:lo + (u + 1) * LANES] = (
                su * (1.0 + lax.erf(su * INV_SQRT2)) * half_gate).astype(BF16)
    y_ref[...] += jnp.dot(w_scr[...], v_ref[...], preferred_element_type=F32)

    @pl.when(c == pl.num_programs(1) - 1)
    def _finish():
        xo = x1_ref[...] + gt2_ref[0] * y_ref[...]
        y_ref[...] = _rms(xo) * gf_ref[...]


def _peer(h2_flat, e1, e2, e2t, gate, ut_bf, v_bf, x1_flat, gt2, g_final, tokens_per_batch, tp):
    t, d = h2_flat.shape
    ec = ut_bf.shape[2]
    nc = v_bf.shape[0] // ec
    nslot = e1.shape[1]
    tiles_per_batch = tokens_per_batch // tp
    tok = lambda i, c: (i, 0)
    once = pl.Buffered(1)
    return pl.pallas_call(
        functools.partial(_peer_kernel, tp=tp, ec=ec),
        grid=(t // tp, nc),
        in_specs=[pl.BlockSpec((tp, d), tok),
                  pl.BlockSpec((tp, nslot), tok),
                  pl.BlockSpec((tp, nslot), tok),
                  pl.BlockSpec((nslot, tp), lambda i, c: (0, i)),
                  pl.BlockSpec((tp, nslot), tok),
                  pl.BlockSpec((1, d, ec), lambda i, c: (c, 0, 0)),
                  pl.BlockSpec((ec, d), lambda i, c: (c, 0)),
                  pl.BlockSpec((tp, d), tok, pipeline_mode=once),
                  pl.BlockSpec((1, 1, d), lambda i, c: (i // tiles_per_batch, 0, 0)),
                  pl.BlockSpec((1, d), lambda i, c: (0, 0))],
        out_specs=pl.BlockSpec((tp, d), tok),
        out_shape=jax.ShapeDtypeStruct((t, d), F32),
        scratch_shapes=[pltpu.VMEM((tp * GATE_PITCH, LANES), I32),
                        pltpu.VMEM((tp, ec), BF16)],
        compiler_params=_cparams(("parallel", "arbitrary")),
        name="peer",
    )(h2_flat, e1, e2, e2t, gate, ut_bf, v_bf, x1_flat, gt2, g_final)


def _rope_tables(s):
    inv_freq = 1.0 / (ROPE_THETA ** (jnp.arange(0, HEAD_DIM, 2, dtype=F32) / HEAD_DIM))
    ang = jnp.arange(s, dtype=F32)[:, None] * inv_freq[None, :]
    ang = jnp.concatenate([ang, ang, ang, ang], axis=-1)
    cos, sin = jnp.cos(ang), jnp.sin(ang)
    low = (jnp.arange(LANES) % HEAD_DIM) < (HEAD_DIM // 2)
    return cos, jnp.where(low, -sin, 0.0), jnp.where(low, 0.0, sin)


def _pick_tile(n, want):
    t = min(n, want)
    while n % t:
        t //= 2
    return t


def _trunk(x, mod, wts, lam_init):
    b, s, d = x.shape
    sh1, sc1, gt1, sh2, sc2, gt2 = [m.reshape(b, 1, d) for m in jnp.split(mod, 6, axis=-1)]
    cos, sa, sb = _rope_tables(s)
    p, q, k, v = _front(x, sc1, sh1, wts["g_norm1"], wts["w_in"], cos, sa, sb, _pick_tile(s, 512))
    mixed = _pool(p, wts["pool_w"], wts["pool_scale"])
    att = _attention(q, k, v, wts["lamv"], wts["g_sub"], lam_init,
                     _pick_tile(s, ATTN_SCORE_BYTES // (2 * 4 * s)),
                     _pick_tile(s, 512))
    x1, h2 = _merge(x, mixed, att, (sc1, sh1, gt1, sc2, sh2), wts["g_norm1"], wts["g_norm2"],
                    wts["w_pool_out"], wts["w_att_out"], wts["w_gate"], wts["b_gate"],
                    wts["w_out"], _pick_tile(s, 512))
    t = b * s
    h2f = h2.reshape(t, d)
    e1, e2, e2t, gate = _route(h2f, wts["w_peer_qt"], wts["peer_keys"])
    y = _peer(h2f, e1, e2, e2t, gate, wts["peer_ut"], wts["peer_v"], x1.reshape(t, d),
              gt2, wts["g_final"], s, _pick_tile(s, PEER_TOKENS))
    return y.reshape(b, s, d)


def kernel(x_prompt, x_sample, c_prompt, c_sample, w_ada, b_ada, g_norm1, w_in, pool_w, pool_scale,
           w_pool_out, lam_q1, lam_k1, lam_q2, lam_k2, g_sub, w_att_out, w_gate, b_gate, w_out,
           g_norm2, w_peer_q, peer_keys, peer_u, peer_v, g_final):
    depth = w_ada.shape[0]
    assert depth == 1, "single-layer trunk"
    l = 0
    lam_init = 0.8 - 0.6 * math.exp(-0.3 * l)
    d = x_prompt.shape[-1]
    wts = {
        "g_norm1": g_norm1[l][None, :], "g_norm2": g_norm2[l][None, :], "g_final": g_final[None, :],
        "w_in": w_in[l].astype(BF16),
        "pool_w": pool_w[l].astype(BF16), "pool_scale": pool_scale[l][None, :],
        "w_pool_out": w_pool_out[l].astype(BF16),
        "lamv": jnp.stack([lam_q1[l], lam_k1[l], lam_q2[l], lam_k2[l]]).astype(F32),
        "g_sub": g_sub[l][None, :],
        "w_att_out": w_att_out[l].astype(BF16),
        "w_gate": w_gate[l].astype(BF16), "b_gate": b_gate[l][None, :],
        "w_out": w_out[l].astype(BF16),
        "w_peer_qt": w_peer_q[l].T.astype(BF16),
        "peer_keys": peer_keys[l].astype(BF16),
        "peer_ut": peer_u[l].astype(BF16).reshape(-1, PEER_CHUNK, d).transpose(0, 2, 1),
        "peer_v": peer_v[l].astype(BF16),
    }
    nb = x_prompt.shape[0]
    mod = _ada(jnp.concatenate([c_prompt, c_sample], axis=0), w_ada[l].astype(BF16), b_ada[l][None, :])
    y_prompt = _trunk(x_prompt, mod[:nb], wts, lam_init)
    y_sample = _trunk(x_sample, mod[nb:], wts, lam_init)
    return (y_prompt, y_sample)
```
